```python
import jax, jax.numpy as jnp
from jax import lax
import numpy as np

D_MODEL = 2048
BATCH = 16
SEQ = 256
DEPTH = 2
DEC_BATCH = 8
DEC_SEQ = 1024
PAST_LEN = 512

GRID_W = 64
H_M = 4
DK_M = 256
DV_M = 256
H_NA = 8
D_NA = 128
MAX_KH = 8
KW = 16
N_EXPERTS = 32
TOP_K = 4
D_FF = 2048
SWIGLU_LIMIT = 7.0
SWIGLU_ALPHA = 1.702
ROPE_BASE = 10000.0
CHUNK = 128
Q_BLOCK = 128
MOE_BLOCK = 128
N_GATES = 4 * H_M
W_M = H_M * DV_M
W_NA = H_NA * D_NA
MIX_W = W_M + W_NA
SPLITS = (H_M * DK_M, H_M * DK_M, W_M, W_M, N_GATES, W_NA, W_NA, W_NA)
N_IN = sum(SPLITS)
ALPHA = (2 * DEPTH) ** 0.25
BETA = (8 * DEPTH) ** -0.25
NEG = -1e30
LN_EPS = 1e-5
HN_EPS = 1e-6

kernel_name = 'hybrid_mlstm_natten_moe_diffusion_step'


def layer_norm(x, w, b):
    xf = x.astype(jnp.float32)
    mu = jnp.mean(xf, axis=-1, keepdims=True)
    var = jnp.mean(jnp.square(xf - mu), axis=-1, keepdims=True)
    return ((xf - mu) * lax.rsqrt(var + LN_EPS) * w + b).astype(x.dtype)


def adaln_params(cvec, w_mod_l, b_mod_l):
    m = jax.nn.silu(cvec) @ w_mod_l + b_mod_l
    return jnp.split(m[..., None, :], 6, axis=-1)


def modulate(x, shift, scale):
    return x * (1 + scale) + shift


def split_projection(h, w_in_l, b_gate_l):
    proj = h @ w_in_l
    bounds = [int(v) for v in np.cumsum(SPLITS)[:-1]]
    mq, mk, mv, mo, g, nq, nk, nv = jnp.split(proj, bounds, axis=-1)
    B, L, _ = h.shape
    gates = (g + b_gate_l).astype(jnp.float32).reshape(B, L, 4, H_M).transpose(2, 0, 3, 1)
    return mq, mk, mv, mo, gates, nq, nk, nv


def axial_rope(x):
    L = x.shape[1]
    pos = jnp.arange(L)
    row = (pos // GRID_W).astype(jnp.float32)
    col = (pos % GRID_W).astype(jnp.float32)
    half = DK_M // 2
    nf = half // 2
    inv = ROPE_BASE ** (-jnp.arange(nf, dtype=jnp.float32) / nf)

    def rot(xp, p):
        ang = p[:, None] * inv
        cos = jnp.cos(ang)[None, :, None, :]
        sin = jnp.sin(ang)[None, :, None, :]
        x1, x2 = xp[..., :nf], xp[..., nf:]
        return jnp.concatenate([x1 * cos - x2 * sin, x1 * sin + x2 * cos], axis=-1)

    xf = x.astype(jnp.float32)
    return jnp.concatenate([rot(xf[..., :half], row), rot(xf[..., half:], col)], axis=-1).astype(x.dtype)


def mlstm_scan(q, k, v, ig, lf, c0, n0, m0):
    B, H, L, _ = q.shape
    nc = L // CHUNK

    def chunks(t):
        return jnp.moveaxis(t.reshape(t.shape[:2] + (nc, CHUNK) + t.shape[3:]), 2, 0)

    causal = jnp.tril(jnp.ones((CHUNK, CHUNK), dtype=bool))

    def step(carry, xs):
        c, n, m = carry
        qc, kc, vc, ic, fc = xs
        b = jnp.cumsum(fc, axis=-1)
        dlog = jnp.where(causal, b[..., :, None] - b[..., None, :] + ic[..., None, :], -jnp.inf)
        g = b + m[..., None]
        mt = jnp.maximum(g, jnp.max(dlog, axis=-1))
        s = jnp.einsum('bhtd,bhsd->bhts', qc, kc) * jnp.exp(dlog - mt[..., None])
        sg = jnp.exp(g - mt)
        num = jnp.einsum('bhts,bhsv->bhtv', s, vc) + sg[..., None] * jnp.einsum('bhtd,bhdv->bhtv', qc, c)
        den = jnp.sum(s, axis=-1) + sg * jnp.einsum('bhtd,bhd->bht', qc, n)
        h = num / jnp.maximum(jnp.abs(den), jnp.exp(-mt))[..., None]
        b_end = b[..., -1]
        wlog = b_end[..., None] - b + ic
        m_new = jnp.maximum(b_end + m, jnp.max(wlog, axis=-1))
        ws = jnp.exp(wlog - m_new[..., None])
        sc = jnp.exp(b_end + m - m_new)
        c_new = sc[..., None, None] * c + jnp.einsum('bhsd,bhsv->bhdv', kc * ws[..., None], vc)
        n_new = sc[..., None] * n + jnp.einsum('bhs,bhsd->bhd', ws, kc)
        return (c_new, n_new, m_new), h

    init = (c0.astype(jnp.float32), n0.astype(jnp.float32), m0.astype(jnp.float32))
    (c, n, m), hs = lax.scan(step, init, (chunks(q), chunks(k), chunks(v), chunks(ig), chunks(lf)))
    h = jnp.moveaxis(hs, 0, 2).reshape(B, H, L, hs.shape[-1])
    return h, (c, n, m)


def empty_state(B):
    return (jnp.zeros((B, H_M, DK_M, DV_M), jnp.float32),
            jnp.zeros((B, H_M, DK_M), jnp.float32),
            jnp.full((B, H_M), NEG, jnp.float32))


def mlstm_mixer(mq, mk, mv, mo, gates, norm_w_l, st_f, st_b, use_rope):
    B, L, _ = mq.shape
    q = mq.reshape(B, L, H_M, DK_M)
    k = mk.reshape(B, L, H_M, DK_M)
    v = mv.reshape(B, L, H_M, DV_M)
    if use_rope:
        q = axial_rope(q)
        k = axial_rope(k)
    q = jnp.transpose(q, (0, 2, 1, 3)).astype(jnp.float32) * (DK_M ** -0.5)
    k = jnp.transpose(k, (0, 2, 1, 3)).astype(jnp.float32)
    v = jnp.transpose(v, (0, 2, 1, 3)).astype(jnp.float32)
    h_f, new_f = mlstm_scan(q, k, v, gates[0], jax.nn.log_sigmoid(gates[1]), *st_f)
    flip = lambda t: jnp.flip(t, axis=2)
    h_b, new_b = mlstm_scan(flip(q), flip(k), flip(v), flip(gates[2]),
                            flip(jax.nn.log_sigmoid(gates[3])), *st_b)
    h = h_f + flip(h_b)
    mu = jnp.mean(h, axis=-1, keepdims=True)
    var = jnp.mean(jnp.square(h - mu), axis=-1, keepdims=True)
    hn = ((h - mu) * lax.rsqrt(var + HN_EPS)).transpose(0, 2, 1, 3).reshape(B, L, W_M) * norm_w_l
    out = jax.nn.sigmoid(mo.astype(jnp.float32)) * hn
    return out.astype(mq.dtype), new_f, new_b


def ctx_attention(q, k, v):
    B, L, H, d = q.shape
    nb = L // Q_BLOCK
    qb = jnp.moveaxis(q.reshape(B, nb, Q_BLOCK, H, d), 1, 0)
    scale = d ** -0.5

    def blk(qi):
        s = jnp.einsum('bqhd,bkhd->bhqk', qi, k).astype(jnp.float32) * scale
        p = jax.nn.softmax(s, axis=-1).astype(v.dtype)
        return jnp.einsum('bhqk,bkhd->bqhd', p, v)

    o = lax.map(blk, qb)
    return jnp.moveaxis(o, 0, 1).reshape(B, L, H * d)


def na_latent(q, k, v, kc, vc, rpb_l):
    B, L, H, d = q.shape
    rows = L // GRID_W
    kh = min(MAX_KH, rows)
    qg = q.reshape(B, rows, GRID_W, H, d)
    kg = k.reshape(B, rows, GRID_W, H, d)
    vg = v.reshape(B, rows, GRID_W, H, d)
    col = jnp.arange(GRID_W)
    cs = jnp.clip(col - KW // 2, 0, GRID_W - KW)
    valid = (col[None, :] >= cs[:, None]) & (col[None, :] < cs[:, None] + KW)
    mask = jnp.broadcast_to(valid[:, None, :], (GRID_W, kh, GRID_W)).reshape(GRID_W, kh * GRID_W)
    cidx = jnp.clip(col[None, :] - col[:, None], -(KW - 1), KW - 1) + (KW - 1)
    scale = d ** -0.5
    n_loc = kh * GRID_W

    def row_fn(r):
        rs = jnp.clip(r - kh // 2, 0, rows - kh)
        qr = lax.dynamic_index_in_dim(qg, r, axis=1, keepdims=False)
        kr = lax.dynamic_slice_in_dim(kg, rs, kh, axis=1).reshape(B, n_loc, H, d)
        vr = lax.dynamic_slice_in_dim(vg, rs, kh, axis=1).reshape(B, n_loc, H, d)
        ridx = rs + jnp.arange(kh) - r + (MAX_KH - 1)
        bias = rpb_l[:, ridx][:, :, cidx]
        bias = bias.transpose(0, 2, 1, 3).reshape(H, GRID_W, n_loc)
        s_loc = jnp.einsum('bqhd,bkhd->bhqk', qr, kr).astype(jnp.float32) * scale + bias
        s_loc = jnp.where(mask, s_loc, -jnp.inf)
        s_ctx = jnp.einsum('bqhd,bkhd->bhqk', qr, kc).astype(jnp.float32) * scale
        p = jax.nn.softmax(jnp.concatenate([s_loc, s_ctx], axis=-1), axis=-1).astype(v.dtype)
        return (jnp.einsum('bhqk,bkhd->bqhd', p[..., :n_loc], vr)
                + jnp.einsum('bhqk,bkhd->bqhd', p[..., n_loc:], vc))

    o = lax.map(row_fn, jnp.arange(rows))
    return jnp.moveaxis(o, 0, 1).reshape(B, L, H * d)


def mixer_context(h, w_in_l, b_gate_l, norm_w_l, w_out_l):
    B, L, _ = h.shape
    mq, mk, mv, mo, gates, nq, nk, nv = split_projection(h, w_in_l, b_gate_l)
    m_out, st_f, st_b = mlstm_mixer(mq, mk, mv, mo, gates, norm_w_l, empty_state(B), empty_state(B), False)
    k_na = nk.reshape(B, L, H_NA, D_NA)
    v_na = nv.reshape(B, L, H_NA, D_NA)
    na_out = ctx_attention(nq.reshape(B, L, H_NA, D_NA), k_na, v_na)
    out = jnp.concatenate([m_out, na_out], axis=-1) @ w_out_l
    return out, k_na, v_na, st_f, st_b


def mixer_latent(h, kc, vc, st_f, st_b, w_in_l, b_gate_l, norm_w_l, rpb_l, w_out_l):
    B, L, _ = h.shape
    mq, mk, mv, mo, gates, nq, nk, nv = split_projection(h, w_in_l, b_gate_l)
    m_out, _, _ = mlstm_mixer(mq, mk, mv, mo, gates, norm_w_l, st_f, st_b, True)
    na_out = na_latent(nq.reshape(B, L, H_NA, D_NA), nk.reshape(B, L, H_NA, D_NA),
                       nv.reshape(B, L, H_NA, D_NA), kc.astype(h.dtype), vc.astype(h.dtype), rpb_l)
    return jnp.concatenate([m_out, na_out], axis=-1) @ w_out_l


def clamped_swiglu(hdn):
    x_glu = jnp.minimum(hdn[..., ::2], SWIGLU_LIMIT)
    x_lin = jnp.clip(hdn[..., 1::2], -SWIGLU_LIMIT, SWIGLU_LIMIT)
    return x_glu * jax.nn.sigmoid(SWIGLU_ALPHA * x_glu) * (x_lin + 1)


def ffn_moe(h, router_w_l, router_b_l, w1_l, b1_l, w2_l, b2_l):
    B, L, D = h.shape
    xf = h.reshape(-1, D)
    T = xf.shape[0]
    logits = (xf @ router_w_l + router_b_l).astype(jnp.float32)
    topv, topi = lax.top_k(logits, TOP_K)
    gw = jax.nn.softmax(topv, axis=-1)
    A = T * TOP_K
    flat_e = topi.reshape(-1)
    flat_g = gw.reshape(-1)
    flat_t = jnp.arange(A, dtype=jnp.int32) // TOP_K
    order = jnp.argsort(flat_e)
    se = flat_e[order]
    counts = jnp.bincount(flat_e, length=N_EXPERTS)
    pcounts = (counts + MOE_BLOCK - 1) // MOE_BLOCK * MOE_BLOCK
    starts = jnp.cumsum(counts) - counts
    pends = jnp.cumsum(pcounts)
    pstarts = pends - pcounts
    dest = pstarts[se] + jnp.arange(A) - starts[se]
    nb = A // MOE_BLOCK + N_EXPERTS
    tok_buf = jnp.zeros((nb * MOE_BLOCK,), jnp.int32).at[dest].set(flat_t[order])
    g_buf = jnp.zeros((nb * MOE_BLOCK,), jnp.float32).at[dest].set(flat_g[order])
    block_e = jnp.clip(jnp.searchsorted(pends, jnp.arange(nb) * MOE_BLOCK, side='right'), 0, N_EXPERTS - 1)

    def block_fn(args):
        tok, g, e = args
        hdn = xf[tok] @ w1_l[e] + b1_l[e]
        out = clamped_swiglu(hdn) @ w2_l[e] + b2_l[e]
        return out * g[:, None].astype(out.dtype)

    outs = lax.map(block_fn, (tok_buf.reshape(nb, MOE_BLOCK), g_buf.reshape(nb, MOE_BLOCK), block_e))
    y = jnp.zeros_like(xf).at[tok_buf].add(outs.reshape(-1, D))
    return y.reshape(B, L, D)


def setup_inputs(seed: int = 0) -> dict:
    key = jax.random.key(seed)
    ks = jax.random.split(key, 24)
    nrm = lambda k, s: jax.random.normal(k, s, jnp.float32)
    col_scale = jnp.asarray(np.concatenate([
        np.ones(SPLITS[0] + SPLITS[1]), np.full(SPLITS[2], BETA), np.ones(SPLITS[3] + SPLITS[4] + SPLITS[5] + SPLITS[6]),
        np.full(SPLITS[7], BETA)]).astype(np.float32))
    gate_base = jnp.repeat(jnp.array([0.0, 3.0, 0.0, 3.0], jnp.float32), H_M)
    return {
        'x_prompt': nrm(ks[0], (BATCH, SEQ, D_MODEL)),
        'x_sample': nrm(ks[1], (DEC_BATCH, DEC_SEQ, D_MODEL)),
        'c': nrm(ks[2], (DEC_BATCH, D_MODEL)),
        'cache_na_k': nrm(ks[3], (DEC_BATCH, DEPTH, PAST_LEN, H_NA, D_NA)),
        'cache_na_v': nrm(ks[4], (DEC_BATCH, DEPTH, PAST_LEN, H_NA, D_NA)),
        'state_mlstm_c': nrm(ks[5], (DEC_BATCH, DEPTH, 2, H_M, DK_M, DV_M)),
        'state_mlstm_n': nrm(ks[6], (DEC_BATCH, DEPTH, 2, H_M, DK_M)),
        'state_mlstm_m': nrm(ks[7], (DEC_BATCH, DEPTH, 2, H_M)),
        'c_ctx': nrm(ks[8], (D_MODEL,)),
        'w_mod': nrm(ks[9], (DEPTH, D_MODEL, 6 * D_MODEL)) * (0.5 * D_MODEL ** -0.5),
        'b_mod': 0.02 * nrm(ks[10], (DEPTH, 6 * D_MODEL)),
        'w_in': nrm(ks[11], (DEPTH, D_MODEL, N_IN)) * (D_MODEL ** -0.5) * col_scale,
        'b_gate': gate_base + 0.1 * nrm(ks[12], (DEPTH, N_GATES)),
        'mlstm_norm_w': 1.0 + 0.02 * nrm(ks[13], (DEPTH, W_M)),
        'na_rpb': 0.1 * nrm(ks[14], (DEPTH, H_NA, 2 * MAX_KH - 1, 2 * KW - 1)),
        'w_out': nrm(ks[15], (DEPTH, MIX_W, D_MODEL)) * (BETA * MIX_W ** -0.5),
        'ln_w': 1.0 + 0.02 * nrm(ks[16], (DEPTH, 2, D_MODEL)),
        'ln_b': 0.02 * nrm(ks[17], (DEPTH, 2, D_MODEL)),
        'router_w': nrm(ks[18], (DEPTH, D_MODEL, N_EXPERTS)) * (D_MODEL ** -0.5),
        'router_b': 0.01 * nrm(ks[19], (DEPTH, N_EXPERTS)),
        'expert_w1': nrm(ks[20], (DEPTH, N_EXPERTS, D_MODEL, 2 * D_FF)) * (D_MODEL ** -0.5),
        'expert_b1': 0.02 * nrm(ks[21], (DEPTH, N_EXPERTS, 2 * D_FF)),
        'expert_w2': nrm(ks[22], (DEPTH, N_EXPERTS, D_FF, D_MODEL)) * (BETA * D_FF ** -0.5),
        'expert_b2': 0.02 * nrm(ks[23], (DEPTH, N_EXPERTS, D_MODEL)),
    }


def reference(x_prompt, x_sample, c, cache_na_k, cache_na_v, state_mlstm_c, state_mlstm_n, state_mlstm_m,
              c_ctx, w_mod, b_mod, w_in, b_gate, mlstm_norm_w, na_rpb, w_out, ln_w, ln_b,
              router_w, router_b, expert_w1, expert_b1, expert_w2, expert_b2):
    x = x_prompt
    ks_l, vs_l, cs_l, ns_l, ms_l = [], [], [], [], []
    for l in range(DEPTH):
        sh_a, sc_a, g_a, sh_f, sc_f, g_f = adaln_params(c_ctx, w_mod[l], b_mod[l])
        out, k_na, v_na, st_f, st_b = mixer_context(modulate(x, sh_a, sc_a), w_in[l], b_gate[l],
                                                    mlstm_norm_w[l], w_out[l])
        x = layer_norm(ALPHA * x + g_a * out, ln_w[l, 0], ln_b[l, 0])
        f_out = ffn_moe(modulate(x, sh_f, sc_f), router_w[l], router_b[l],
                        expert_w1[l], expert_b1[l], expert_w2[l], expert_b2[l])
        x = layer_norm(ALPHA * x + g_f * f_out, ln_w[l, 1], ln_b[l, 1])
        ks_l.append(k_na)
        vs_l.append(v_na)
        cs_l.append(jnp.stack([st_f[0], st_b[0]], axis=1))
        ns_l.append(jnp.stack([st_f[1], st_b[1]], axis=1))
        ms_l.append(jnp.stack([st_f[2], st_b[2]], axis=1))
    y_prompt = x
    new_cache_na_k = jnp.stack(ks_l, axis=1)
    new_cache_na_v = jnp.stack(vs_l, axis=1)
    new_state_mlstm_c = jnp.stack(cs_l, axis=1)
    new_state_mlstm_n = jnp.stack(ns_l, axis=1)
    new_state_mlstm_m = jnp.stack(ms_l, axis=1)

    x = x_sample
    for l in range(DEPTH):
        sh_a, sc_a, g_a, sh_f, sc_f, g_f = adaln_params(c, w_mod[l], b_mod[l])
        st_f = (state_mlstm_c[:, l, 0], state_mlstm_n[:, l, 0], state_mlstm_m[:, l, 0])
        st_b = (state_mlstm_c[:, l, 1], state_mlstm_n[:, l, 1], state_mlstm_m[:, l, 1])
        out = mixer_latent(modulate(x, sh_a, sc_a), cache_na_k[:, l], cache_na_v[:, l], st_f, st_b,
                           w_in[l], b_gate[l], mlstm_norm_w[l], na_rpb[l], w_out[l])
        x = layer_norm(ALPHA * x + g_a * out, ln_w[l, 0], ln_b[l, 0])
        f_out = ffn_moe(modulate(x, sh_f, sc_f), router_w[l], router_b[l],
                        expert_w1[l], expert_b1[l], expert_w2[l], expert_b2[l])
        x = layer_norm(ALPHA * x + g_f * f_out, ln_w[l, 1], ln_b[l, 1])
    y_sample = x
    return (y_prompt, y_sample, new_cache_na_k, new_cache_na_v, new_state_mlstm_c, new_state_mlstm_n, new_state_mlstm_m)
```

```python
import functools

import numpy as np
import jax
import jax.numpy as jnp
from jax import lax
from jax.experimental import pallas as pl
from jax.experimental.pallas import tpu as pltpu

F32 = jnp.float32
BF16 = jnp.bfloat16

D_MODEL = 2048
BATCH, SEQ = 16, 256
DEC_BATCH, DEC_SEQ = 8, 1024
DEPTH = 2
PAST_LEN = 512
GRID_W = 64
H_M, DK_M, DV_M = 4, 256, 256
H_NA, D_NA = 8, 128
MAX_KH, KW = 8, 16
N_EXPERTS, TOP_K, D_FF = 32, 4, 2048
SWIGLU_LIMIT, SWIGLU_ALPHA = 7.0, 1.702
ROPE_BASE = 10000.0
CHUNK = 128
N_GATES = 4 * H_M
W_M = H_M * DV_M
W_NA = H_NA * D_NA
ALPHA = (2 * DEPTH) ** 0.25
NEG = -1e30
LN_EPS = 1e-5
HN_EPS = 1e-6

T_CTX = BATCH * SEQ
T_LAT = DEC_BATCH * DEC_SEQ
T_ALL = T_CTX + T_LAT
N_MAIN = 4 * W_M + 3 * W_NA
N_ASSIGN = T_ALL * TOP_K

LANES = 128
VMEM_LIMIT = 56 * 1024 * 1024

TM_IN = 512
TN_IN = 1024
TM_OUT = 256
TM_MOE = 256
CH_MOE = 2048
TPC = CH_MOE // TM_MOE
TF = 256
NJ = D_FF // TF
ROWS_SORTED = N_ASSIGN + N_EXPERTS * TM_MOE
NT_SORTED = ROWS_SORTED // TM_MOE
NC_MAX = N_EXPERTS + -(-NT_SORTED // TPC)
RB_GATHER = 512
TM_COMB = 128


def _cparams(n_axes, vmem=VMEM_LIMIT):
    return pltpu.CompilerParams(dimension_semantics=("arbitrary",) * n_axes, vmem_limit_bytes=vmem)


def _dot(a, b):
    return jnp.dot(a, b, preferred_element_type=F32)


def _split2(x):
    hi = x.astype(BF16)
    lo = (x - hi.astype(F32)).astype(BF16)
    return hi, lo


def _dot_f32(a, b):
    a1, a2 = _split2(a)
    b1, b2 = _split2(b)
    return _dot(a1, b1) + (_dot(a1, b2) + _dot(a2, b1))


def _layer_norm(z, w, b):
    mu = jnp.mean(z, axis=-1, keepdims=True)
    zc = z - mu
    var = jnp.mean(zc * zc, axis=-1, keepdims=True)
    return zc * lax.rsqrt(var + LN_EPS) * w + b


MOD_ROWS = 16
TN_MOD = 512


def _adaln_kernel(c_ref, w_ref, b_ref, o_ref):
    c = c_ref[...]
    s = c * jax.nn.sigmoid(c)
    o_ref[...] = _dot_f32(s, w_ref[...]) + b_ref[...]


def _adaln(cvec, w_mod, b_mod):
    n = 6 * D_MODEL
    return pl.pallas_call(
        _adaln_kernel,
        out_shape=jax.ShapeDtypeStruct((DEPTH, MOD_ROWS, n), F32),
        grid=(DEPTH, n // TN_MOD),
        in_specs=[
            pl.BlockSpec((MOD_ROWS, D_MODEL), lambda l, j: (0, 0)),
            pl.BlockSpec((None, D_MODEL, TN_MOD), lambda l, j: (l, 0, j)),
            pl.BlockSpec((None, 1, TN_MOD), lambda l, j: (l, 0, j)),
        ],
        out_specs=pl.BlockSpec((None, MOD_ROWS, TN_MOD), lambda l, j: (l, 0, j)),
        compiler_params=_cparams(2),
        name="adaln",
    )(cvec, w_mod, b_mod.reshape(DEPTH, 1, n))


def _mod_spec(layer, k, rows_per_tile):
    ctx_tiles = T_CTX // rows_per_tile
    tiles_per_lat = DEC_SEQ // rows_per_tile

    def index(i, *_):
        r = jnp.where(i < ctx_tiles, 0, 1 + (i - ctx_tiles) // tiles_per_lat)
        return ((layer * MOD_ROWS + r) * 6 + k, 0, 0)

    return pl.BlockSpec((None, 1, D_MODEL), index)


GATE_W = H_M * LANES


def _inproj_kernel(x_ref, sh_ref, sc_ref, w_ref, wg_ref, bg_ref, o_ref, g_ref, h_scr):
    @pl.when(pl.program_id(1) == 0)
    def _():
        h = (x_ref[...] * (1.0 + sc_ref[...]) + sh_ref[...]).astype(BF16)
        h_scr[...] = h
        g_ref[...] = _dot(h, wg_ref[...]) + bg_ref[...]

    o_ref[...] = _dot(h_scr[...], w_ref[...])


def _in_proj(layer, x, mods, w_main, w_gate, b_gate):
    return pl.pallas_call(
        _inproj_kernel,
        out_shape=(jax.ShapeDtypeStruct((T_ALL, N_MAIN), F32),
                   jax.ShapeDtypeStruct((T_ALL, GATE_W), F32)),
        grid=(T_ALL // TM_IN, N_MAIN // TN_IN),
        in_specs=[
            pl.BlockSpec((TM_IN, D_MODEL), lambda i, j: (i, 0)),
            _mod_spec(layer, 0, TM_IN),
            _mod_spec(layer, 1, TM_IN),
            pl.BlockSpec((None, D_MODEL, TN_IN), lambda i, j: (layer, 0, j)),
            pl.BlockSpec((None, D_MODEL, GATE_W), lambda i, j: (layer, 0, 0)),
            pl.BlockSpec((None, 1, GATE_W), lambda i, j: (layer, 0, 0)),
        ],
        out_specs=(pl.BlockSpec((TM_IN, TN_IN), lambda i, j: (i, j)),
                   pl.BlockSpec((TM_IN, GATE_W), lambda i, j: (i, 0))),
        scratch_shapes=[pltpu.VMEM((TM_IN, D_MODEL), BF16)],
        compiler_params=_cparams(2),
        name="in_proj",
    )(x, mods, mods, w_main, w_gate, b_gate)


def _log_sigmoid(x):
    return jnp.minimum(x, 0.0) - jnp.log(1.0 + jnp.exp(-jnp.abs(x)))


def _mlstm_kernel(*refs, seq, use_rope, has_init, emit_state, layer):
    it = iter(refs)
    q_ref, k_ref, v_ref, mo_ref, g_ref, nw_ref = (next(it) for _ in range(6))
    if use_rope:
        cos_ref, sin_ref = next(it), next(it)
    if has_init:
        c0_ref, n0_ref, m0_ref = next(it), next(it), next(it)
    out_ref = next(it)
    if emit_state:
        co_ref, no_ref, mso_ref = next(it), next(it), next(it)
    qs, ks, hs, c_scr = next(it), next(it), next(it), next(it)

    nc = seq // CHUNK
    b_idx = pl.program_id(0)
    h_idx = pl.program_id(1)

    def prep(j, carry):
        r0 = pl.multiple_of(j * CHUNK, CHUNK)
        rows = pl.ds(r0, CHUNK)
        q = q_ref[rows, :]
        k = k_ref[rows, :]
        if use_rope:
            cos = cos_ref[rows, :]
            sin = sin_ref[rows, :]

            def rope(x):
                halves = []
                for p in range(2):
                    xp = x[:, p * LANES:(p + 1) * LANES]
                    halves.append(xp * cos[:, p * LANES:(p + 1) * LANES]
                                  + pltpu.roll(xp, LANES // 2, axis=1) * sin[:, p * LANES:(p + 1) * LANES])
                return jnp.concatenate(halves, axis=1)

            q = rope(q)
            k = rope(k)
        qs[rows, :] = (q * (DK_M ** -0.5)).astype(BF16)
        ks[rows, :] = k.astype(BF16)
        return carry

    lax.fori_loop(0, nc, prep, 0)

    row_i = lax.broadcasted_iota(jnp.int32, (CHUNK, CHUNK), 0)
    col_i = lax.broadcasted_iota(jnp.int32, (CHUNK, CHUNK), 1)

    for d in range(2):
        mask = (col_i <= row_i) if d == 0 else (col_i >= row_i)
        tri = mask.astype(BF16)
        lane_i, lane_b = 2 * d, 2 * d + 1
        end_row = CHUNK - 1 if d == 0 else 0

        if has_init:
            c_scr[...] = c0_ref[d]
            n_init = n0_ref[d]
            m_init = jnp.full((1, 1), m0_ref[((b_idx * DEPTH + layer) * 2 + d) * H_M + h_idx], F32)
        else:
            c_scr[...] = jnp.zeros((DK_M, DV_M), F32)
            n_init = jnp.zeros((1, DK_M), F32)
            m_init = jnp.full((1, 1), NEG, F32)

        def step(jj, carry, d=d, mask=mask, tri=tri, lane_i=lane_i, lane_b=lane_b, end_row=end_row):
            n, m = carry
            j = jj if d == 0 else nc - 1 - jj
            r0 = pl.multiple_of(j * CHUNK, CHUNK)
            rows = pl.ds(r0, CHUNK)
            gates = g_ref[rows, :]
            lf = _log_sigmoid(gates)
            lf1, lf2 = _split2(lf)
            lf3 = (lf - lf1.astype(F32) - lf2.astype(F32)).astype(BF16)
            bc = _dot(tri, lf1) + (_dot(tri, lf2) + _dot(tri, lf3))
            gates_t = gates.T
            bc_t = bc.T
            b_col = bc[:, lane_b:lane_b + 1]
            ig_col = gates[:, lane_i:lane_i + 1]
            b_row = bc_t[lane_b:lane_b + 1, :]
            ig_row = gates_t[lane_i:lane_i + 1, :]
            b_end = bc[end_row:end_row + 1, lane_b:lane_b + 1]

            dlog = jnp.where(mask, (b_col - b_row) + ig_row, NEG)
            g = b_col + m
            mt = jnp.maximum(g, jnp.max(dlog, axis=1, keepdims=True))
            qc = qs[rows, :]
            kc = ks[rows, :]
            vc = v_ref[rows, :].astype(BF16)
            s = lax.dot_general(qc, kc, (((1,), (1,)), ((), ())), preferred_element_type=F32)
            s = s * jnp.exp(dlog - mt)
            sg = jnp.exp(g - mt)
            c_old = c_scr[...]
            num = _dot(s.astype(BF16), vc) + sg * _dot(qc, c_old.astype(BF16))
            den = (jnp.sum(s, axis=1, keepdims=True)
                   + sg * jnp.sum(qc.astype(F32) * n, axis=1, keepdims=True))
            h = num / jnp.maximum(jnp.abs(den), jnp.exp(-mt))

            wlog = (b_end - b_col) + ig_col
            m_new = jnp.maximum(b_end + m, jnp.max(wlog, axis=0, keepdims=True))
            ws = jnp.exp(wlog - m_new)
            sc = jnp.exp(b_end + m - m_new)
            kw = kc.astype(F32) * ws
            c_scr[...] = sc * c_old + _dot(kw.T.astype(BF16), vc)
            n_new = sc * n + jnp.sum(kw, axis=0, keepdims=True)

            if d == 0:
                hs[rows, :] = h
            else:
                ht = hs[rows, :] + h
                mu = jnp.mean(ht, axis=1, keepdims=True)
                hc = ht - mu
                var = jnp.mean(hc * hc, axis=1, keepdims=True)
                hn = hc * lax.rsqrt(var + HN_EPS) * nw_ref[...]
                out_ref[rows, :] = (jax.nn.sigmoid(mo_ref[rows, :]) * hn).astype(out_ref.dtype)
            return n_new, m_new

        n_fin, m_fin = lax.fori_loop(0, nc, step, (n_init, m_init))
        if emit_state:
            co_ref[d] = c_scr[...]
            no_ref[d] = n_fin
            mso_ref[d] = jnp.broadcast_to(m_fin, (1, LANES))


def _rope_tables(seq):
    pos = np.arange(seq)
    half = DK_M // 2
    nf = half // 2
    inv = ROPE_BASE ** (-np.arange(nf, dtype=np.float32) / nf)
    cos_parts, sin_parts = [], []
    for p in ((pos // GRID_W).astype(np.float32), (pos % GRID_W).astype(np.float32)):
        ang = (p[:, None] * inv).astype(np.float32)
        cos_parts += [np.cos(ang), np.cos(ang)]
        sin_parts += [-np.sin(ang), np.sin(ang)]
    return (jnp.asarray(np.concatenate(cos_parts, axis=1), F32),
            jnp.asarray(np.concatenate(sin_parts, axis=1), F32))


def _mlstm(layer, proj, gates, norm_w, *, latent, state=None):
    if latent:
        nb, seq, rb0 = DEC_BATCH, DEC_SEQ, T_CTX // DEC_SEQ
    else:
        nb, seq, rb0 = BATCH, SEQ, 0
    cb = W_M // DK_M
    in_specs = [
        pl.BlockSpec((seq, DK_M), lambda b, h: (rb0 + b, h)),
        pl.BlockSpec((seq, DK_M), lambda b, h: (rb0 + b, cb + h)),
        pl.BlockSpec((seq, DV_M), lambda b, h: (rb0 + b, 2 * cb + h)),
        pl.BlockSpec((seq, DV_M), lambda b, h: (rb0 + b, 3 * cb + h)),
        pl.BlockSpec((seq, LANES), lambda b, h: (rb0 + b, h)),
        pl.BlockSpec((None, 1, DV_M), lambda b, h: (layer, 0, h)),
    ]
    args = [proj, proj, proj, proj, gates, norm_w.reshape(DEPTH, 1, W_M)]
    if latent:
        cos, sin = _rope_tables(seq)
        in_specs += [pl.BlockSpec((seq, DK_M), lambda b, h: (0, 0))] * 2
        args += [cos, sin]
        st_c, st_n, st_m = state
        in_specs += [
            pl.BlockSpec((None, None, 2, None, DK_M, DV_M), lambda b, h: (b, layer, 0, h, 0, 0)),
            pl.BlockSpec((None, None, 2, None, 1, DK_M), lambda b, h: (b, layer, 0, h, 0, 0)),
            pl.BlockSpec(memory_space=pltpu.SMEM),
        ]
        args += [st_c, st_n.reshape(DEC_BATCH, DEPTH, 2, H_M, 1, DK_M), st_m.reshape(-1)]
    out_shape = [jax.ShapeDtypeStruct((nb * seq, W_M), BF16)]
    out_specs = [pl.BlockSpec((seq, DV_M), lambda b, h: (b, h))]
    if not latent:
        out_shape += [
            jax.ShapeDtypeStruct((nb, 2, H_M, DK_M, DV_M), F32),
            jax.ShapeDtypeStruct((nb, 2, H_M, 1, DK_M), F32),
            jax.ShapeDtypeStruct((nb, 2, H_M, 1, LANES), F32),
        ]
        out_specs += [
            pl.BlockSpec((None, 2, None, DK_M, DV_M), lambda b, h: (b, 0, h, 0, 0)),
            pl.BlockSpec((None, 2, None, 1, DK_M), lambda b, h: (b, 0, h, 0, 0)),
            pl.BlockSpec((None, 2, None, 1, LANES), lambda b, h: (b, 0, h, 0, 0)),
        ]
    kern = functools.partial(_mlstm_kernel, seq=seq, use_rope=latent, has_init=latent,
                             emit_state=not latent, layer=layer)
    return pl.pallas_call(
        kern,
        out_shape=tuple(out_shape),
        grid=(nb, H_M),
        in_specs=in_specs,
        out_specs=tuple(out_specs),
        scratch_shapes=[
            pltpu.VMEM((seq, DK_M), BF16),
            pltpu.VMEM((seq, DK_M), BF16),
            pltpu.VMEM((seq, DV_M), F32),
            pltpu.VMEM((DK_M, DV_M), F32),
        ],
        compiler_params=_cparams(2),
        name="mlstm_lat" if latent else "mlstm_ctx",
    )(*args)


RPB_ROWS = 2 * MAX_KH - 1
RPB_COLS = 2 * KW - 1
Q_TILE = 256
LAT_ROWS = DEC_SEQ // GRID_W
KH = min(MAX_KH, LAT_ROWS)


def _attn_ctx_kernel(q_ref, k_ref, v_ref, o_ref):
    scale = D_NA ** -0.5
    q = q_ref[...].astype(BF16)
    k = k_ref[...].astype(BF16)
    v = v_ref[...].astype(BF16)
    s = lax.dot_general(q, k, (((1,), (1,)), ((), ())), preferred_element_type=F32) * scale
    m = jnp.max(s, axis=1, keepdims=True)
    p = jnp.exp(s - m)
    l = jnp.sum(p, axis=1, keepdims=True)
    o_ref[...] = (_dot(p.astype(BF16), v) / l).astype(o_ref.dtype)


def _attn_lat_kernel(rpb_ref, q_ref, k_ref, v_ref, kc_ref, vc_ref, o_ref, bias_scr, g_scr):
    h_idx = pl.program_id(0)
    scale = D_NA ** -0.5

    @pl.when(pl.program_id(1) == 0)
    def _build_bias():
        qc = lax.broadcasted_iota(jnp.int32, (GRID_W, LANES), 0)
        lane = lax.broadcasted_iota(jnp.int32, (GRID_W, LANES), 1)
        kc = lane & (GRID_W - 1)
        upper = lane >= GRID_W
        cidx = jnp.clip(kc - qc, -(KW - 1), KW - 1) + (KW - 1)
        cs = jnp.clip(qc - KW // 2, 0, GRID_W - KW)
        col_ok = (kc >= cs) & (kc < cs + KW)
        base = h_idx * (RPB_ROWS * RPB_COLS)
        for dr0 in range(-KH, KH):
            acc = jnp.zeros((GRID_W, LANES), F32)
            for c in range(RPB_COLS):
                lo = rpb_ref[base + (dr0 + MAX_KH - 1) * RPB_COLS + c] if -MAX_KH < dr0 < MAX_KH else 0.0
                hi = rpb_ref[base + (dr0 + MAX_KH) * RPB_COLS + c] if -MAX_KH < dr0 + 1 < MAX_KH else 0.0
                acc = jnp.where(cidx == c, jnp.where(upper, hi, lo), acc)
            g_scr[dr0 + KH] = jnp.where(col_ok, acc, NEG)
        neg_tile = jnp.full((GRID_W, LANES), NEG, F32)
        for qr in range(LAT_ROWS):
            rs = min(max(qr - KH // 2, 0), LAT_ROWS - KH)
            for t in range(LAT_ROWS // 2):
                ok0 = rs <= 2 * t < rs + KH
                ok1 = rs <= 2 * t + 1 < rs + KH
                if ok0 or ok1:
                    tile = g_scr[2 * t - qr + KH]
                    if not ok1:
                        tile = jnp.where(upper, NEG, tile)
                    if not ok0:
                        tile = jnp.where(upper, tile, NEG)
                else:
                    tile = neg_tile
                bias_scr[qr * GRID_W:(qr + 1) * GRID_W, t * LANES:(t + 1) * LANES] = tile

    k = k_ref[...].astype(BF16)
    v = v_ref[...].astype(BF16)
    kc = kc_ref[...].astype(BF16)
    vc = vc_ref[...].astype(BF16)
    dn = (((1,), (1,)), ((), ()))
    for qt in range(DEC_SEQ // Q_TILE):
        rows = slice(qt * Q_TILE, (qt + 1) * Q_TILE)
        q = q_ref[rows, :].astype(BF16)
        s_loc = lax.dot_general(q, k, dn, preferred_element_type=F32) * scale + bias_scr[rows, :]
        s_ctx = lax.dot_general(q, kc, dn, preferred_element_type=F32) * scale
        m = jnp.maximum(jnp.max(s_loc, axis=1, keepdims=True), jnp.max(s_ctx, axis=1, keepdims=True))
        p_loc = jnp.exp(s_loc - m)
        p_ctx = jnp.exp(s_ctx - m)
        l = jnp.sum(p_loc, axis=1, keepdims=True) + jnp.sum(p_ctx, axis=1, keepdims=True)
        o = _dot(p_loc.astype(BF16), v) + _dot(p_ctx.astype(BF16), vc)
        o_ref[rows, :] = (o / l).astype(o_ref.dtype)


def _attention(layer, proj, *, latent, cache_k=None, cache_v=None, rpb=None):
    qb0 = 4 * W_M // D_NA
    if not latent:
        return pl.pallas_call(
            _attn_ctx_kernel,
            out_shape=jax.ShapeDtypeStruct((T_CTX, W_NA), BF16),
            grid=(BATCH, H_NA),
            in_specs=[
                pl.BlockSpec((SEQ, D_NA), lambda b, h: (b, qb0 + h)),
                pl.BlockSpec((SEQ, D_NA), lambda b, h: (b, qb0 + H_NA + h)),
                pl.BlockSpec((SEQ, D_NA), lambda b, h: (b, qb0 + 2 * H_NA + h)),
            ],
            out_specs=pl.BlockSpec((SEQ, D_NA), lambda b, h: (b, h)),
            compiler_params=_cparams(2),
            name="attn_ctx",
        )(proj, proj, proj)
    rb0 = T_CTX // DEC_SEQ
    ck = cache_k.reshape(DEC_BATCH, DEPTH, PAST_LEN, W_NA)
    cv = cache_v.reshape(DEC_BATCH, DEPTH, PAST_LEN, W_NA)
    grid_spec = pltpu.PrefetchScalarGridSpec(
        num_scalar_prefetch=1,
        grid=(H_NA, DEC_BATCH),
        in_specs=[
            pl.BlockSpec((DEC_SEQ, D_NA), lambda h, b, r: (rb0 + b, qb0 + h)),
            pl.BlockSpec((DEC_SEQ, D_NA), lambda h, b, r: (rb0 + b, qb0 + H_NA + h)),
            pl.BlockSpec((DEC_SEQ, D_NA), lambda h, b, r: (rb0 + b, qb0 + 2 * H_NA + h)),
            pl.BlockSpec((None, None, PAST_LEN, D_NA), lambda h, b, r: (b, layer, 0, h)),
            pl.BlockSpec((None, None, PAST_LEN, D_NA), lambda h, b, r: (b, layer, 0, h)),
        ],
        out_specs=pl.BlockSpec((DEC_SEQ, D_NA), lambda h, b, r: (b, h)),
        scratch_shapes=[
            pltpu.VMEM((DEC_SEQ, DEC_SEQ), F32),
            pltpu.VMEM((2 * KH, GRID_W, LANES), F32),
        ],
    )
    return pl.pallas_call(
        _attn_lat_kernel,
        out_shape=jax.ShapeDtypeStruct((T_LAT, W_NA), BF16),
        grid_spec=grid_spec,
        compiler_params=_cparams(2),
        name="attn_lat",
    )(rpb[layer].reshape(-1), proj, proj, proj, ck, cv)


def _outproj_kernel(mc_ref, ml_ref, ac_ref, al_ref, w_ref, x_ref, ga_ref, shf_ref, scf_ref,
                    lnw_ref, lnb_ref, rw_ref, rb_ref, x1_ref, h2_ref, ti_ref, tg_ref):
    is_ctx = pl.program_id(0) < T_CTX // TM_OUT
    m = jnp.where(is_ctx, mc_ref[...], ml_ref[...])
    a = jnp.where(is_ctx, ac_ref[...], al_ref[...])
    y = _dot(m, w_ref[0:W_M, :]) + _dot(a, w_ref[W_M:W_M + W_NA, :])
    x1 = _layer_norm(ALPHA * x_ref[...] + ga_ref[...] * y, lnw_ref[...], lnb_ref[...])
    x1_ref[...] = x1
    h2 = x1 * (1.0 + scf_ref[...]) + shf_ref[...]
    h2_ref[...] = h2
    logits = _dot_f32(h2, rw_ref[...]) + rb_ref[...]
    lane = lax.broadcasted_iota(jnp.int32, logits.shape, 1)
    vals = logits
    top_v, top_i = [], []
    for _ in range(TOP_K):
        mx = jnp.max(vals, axis=1, keepdims=True)
        ix = jnp.min(jnp.where(vals == mx, lane, LANES), axis=1, keepdims=True)
        top_v.append(mx)
        top_i.append(ix)
        vals = jnp.where(lane == ix, -jnp.inf, vals)
    ex = [jnp.exp(v - top_v[0]) for v in top_v]
    den = ex[0] + ex[1] + ex[2] + ex[3]
    ti = jnp.zeros(logits.shape, jnp.int32)
    tg = jnp.zeros(logits.shape, F32)
    for kk in range(TOP_K):
        ti = jnp.where(lane == kk, top_i[kk], ti)
        tg = jnp.where(lane == kk, ex[kk] / den, tg)
    ti_ref[...] = ti
    tg_ref[...] = tg


def _out_proj(layer, m_ctx, m_lat, a_ctx, a_lat, w_out, x, mods, ln_w, ln_b, router_w, router_b):
    nctx = T_CTX // TM_OUT
    ctx_spec = lambda w: pl.BlockSpec((TM_OUT, w), lambda i: (jnp.minimum(i, nctx - 1), 0))
    lat_spec = lambda w: pl.BlockSpec((TM_OUT, w), lambda i: (jnp.maximum(i - nctx, 0), 0))
    row_spec = lambda w: pl.BlockSpec((TM_OUT, w), lambda i: (i, 0))
    return pl.pallas_call(
        _outproj_kernel,
        out_shape=(jax.ShapeDtypeStruct((T_ALL, D_MODEL), F32),
                   jax.ShapeDtypeStruct((T_ALL, D_MODEL), F32),
                   jax.ShapeDtypeStruct((T_ALL, LANES), jnp.int32),
                   jax.ShapeDtypeStruct((T_ALL, LANES), F32)),
        grid=(T_ALL // TM_OUT,),
        in_specs=[
            ctx_spec(W_M), lat_spec(W_M), ctx_spec(W_NA), lat_spec(W_NA),
            pl.BlockSpec((None, W_M + W_NA, D_MODEL), lambda i: (layer, 0, 0)),
            row_spec(D_MODEL),
            _mod_spec(layer, 2, TM_OUT), _mod_spec(layer, 3, TM_OUT), _mod_spec(layer, 4, TM_OUT),
            pl.BlockSpec((None, None, 1, D_MODEL), lambda i: (layer, 0, 0, 0)),
            pl.BlockSpec((None, None, 1, D_MODEL), lambda i: (layer, 0, 0, 0)),
            pl.BlockSpec((None, D_MODEL, LANES), lambda i: (layer, 0, 0)),
            pl.BlockSpec((None, 1, LANES), lambda i: (layer, 0, 0)),
        ],
        out_specs=(row_spec(D_MODEL), row_spec(D_MODEL), row_spec(LANES), row_spec(LANES)),
        compiler_params=_cparams(1),
        name="out_proj",
    )(m_ctx, m_lat, a_ctx, a_lat, w_out, x, mods, mods, mods,
      ln_w.reshape(DEPTH, 2, 1, D_MODEL), ln_b.reshape(DEPTH, 2, 1, D_MODEL), router_w, router_b)


def _gather_kernel(tok_ref, src_ref, o_ref, sem):
    def start(j, carry):
        pltpu.make_async_copy(src_ref.at[pl.ds(tok_ref[0, j], 1), :], o_ref.at[pl.ds(j, 1), :], sem).start()
        return carry

    def wait(j, carry):
        pltpu.make_async_copy(src_ref.at[pl.ds(0, 1), :], o_ref.at[pl.ds(0, 1), :], sem).wait()
        return carry

    lax.fori_loop(0, RB_GATHER, start, 0)
    lax.fori_loop(0, RB_GATHER, wait, 0)


def _gather_rows(src, tok_sorted):
    nblk = ROWS_SORTED // RB_GATHER
    return pl.pallas_call(
        _gather_kernel,
        out_shape=jax.ShapeDtypeStruct((ROWS_SORTED, D_MODEL), F32),
        grid=(nblk,),
        in_specs=[
            pl.BlockSpec((None, 1, RB_GATHER), lambda i: (i, 0, 0), memory_space=pltpu.SMEM),
            pl.BlockSpec(memory_space=pl.ANY),
        ],
        out_specs=pl.BlockSpec((RB_GATHER, D_MODEL), lambda i: (i, 0)),
        scratch_shapes=[pltpu.SemaphoreType.DMA(())],
        compiler_params=_cparams(1),
        name="moe_gather",
    )(tok_sorted.reshape(nblk, 1, RB_GATHER), src)


def _moe_kernel(ce_ref, cs_ref, cn_ref, nch_ref, xs_ref, w1_ref, b1_ref, w2_ref, b2_ref, sel_ref,
                out_ref, xbuf, acc, w1b, w2b, stage, sem_in, sem_out):
    c = pl.program_id(0)
    j = pl.program_id(1)
    nt = cn_ref[c]
    row0 = cs_ref[c] * TM_MOE

    def tile_rows(r):
        return pl.ds(pl.multiple_of(r * TM_MOE, TM_MOE), TM_MOE)

    @pl.when(nt > 0)
    def _chunk():
        @pl.when(j == 0)
        def _load_rows():
            def load(r, carry):
                src = xs_ref.at[pl.ds(pl.multiple_of(row0 + r * TM_MOE, TM_MOE), TM_MOE), :]
                cp = pltpu.make_async_copy(src, stage, sem_in)
                cp.start()
                cp.wait()
                xbuf[tile_rows(r), :] = stage[...].astype(BF16)
                acc[tile_rows(r), :] = jnp.broadcast_to(b2_ref[...], (TM_MOE, D_MODEL))
                return carry

            lax.fori_loop(0, nt, load, 0)

        w1b[...] = w1_ref[...].astype(BF16)
        w2b[...] = w2_ref[...].astype(BF16)

        def tile(r, carry):
            rows = tile_rows(r)
            hdn = _dot(xbuf[rows, :], w1b[...]) + b1_ref[...]
            glu = jnp.minimum(hdn, SWIGLU_LIMIT)
            lin = jnp.clip(hdn, -SWIGLU_LIMIT, SWIGLU_LIMIT) + 1.0
            z = glu * jax.nn.sigmoid(SWIGLU_ALPHA * glu) * pltpu.roll(lin, 2 * TF - 1, axis=1)
            act = _dot(z.astype(BF16), sel_ref[...]).astype(BF16)
            acc[rows, :] = acc[rows, :] + _dot(act, w2b[...])
            return carry

        lax.fori_loop(0, nt, tile, 0)

        @pl.when(j == NJ - 1)
        def _store_rows():
            def copy(r):
                dst = out_ref.at[pl.ds(pl.multiple_of(row0 + r * TM_MOE, TM_MOE), TM_MOE), :]
                return pltpu.make_async_copy(acc.at[tile_rows(r), :], dst, sem_out)

            def start(r, carry):
                copy(r).start()
                return carry

            def wait(r, carry):
                copy(r).wait()
                return carry

            lax.fori_loop(0, nt, start, 0)
            lax.fori_loop(0, nt, wait, 0)

    @pl.when((c == NC_MAX - 1) & (j == NJ - 1))
    def _zero_unused_tiles():
        stage[...] = jnp.zeros((TM_MOE, D_MODEL), F32)

        def copy(t):
            dst = out_ref.at[pl.ds(pl.multiple_of(t * TM_MOE, TM_MOE), TM_MOE), :]
            return pltpu.make_async_copy(stage, dst, sem_out)

        def start(t, carry):
            copy(t).start()
            return carry

        def wait(t, carry):
            copy(t).wait()
            return carry

        lax.fori_loop(nch_ref[1], NT_SORTED, start, 0)
        lax.fori_loop(nch_ref[1], NT_SORTED, wait, 0)


def _moe_experts(layer, xs, tables, w1, b1, w2, b2):
    ce, cs, cn, nch = tables
    sel = np.zeros((2 * TF, TF), np.float32)
    sel[2 * np.arange(TF), np.arange(TF)] = 1.0

    def jeff(c, j, nch_ref):
        return jnp.where(c < nch_ref[0], j, NJ - 1)

    grid_spec = pltpu.PrefetchScalarGridSpec(
        num_scalar_prefetch=4,
        grid=(NC_MAX, NJ),
        in_specs=[
            pl.BlockSpec(memory_space=pl.ANY),
            pl.BlockSpec((None, None, D_MODEL, 2 * TF),
                         lambda c, j, ce, cs, cn, nch: (layer, ce[c], 0, jeff(c, j, nch))),
            pl.BlockSpec((None, None, 1, 2 * TF),
                         lambda c, j, ce, cs, cn, nch: (layer, ce[c], 0, jeff(c, j, nch))),
            pl.BlockSpec((None, None, TF, D_MODEL),
                         lambda c, j, ce, cs, cn, nch: (layer, ce[c], jeff(c, j, nch), 0)),
            pl.BlockSpec((None, None, 1, D_MODEL),
                         lambda c, j, ce, cs, cn, nch: (layer, ce[c], 0, 0)),
            pl.BlockSpec((2 * TF, TF), lambda c, j, ce, cs, cn, nch: (0, 0)),
        ],
        out_specs=pl.BlockSpec(memory_space=pl.ANY),
        scratch_shapes=[
            pltpu.VMEM((CH_MOE, D_MODEL), BF16),
            pltpu.VMEM((CH_MOE, D_MODEL), F32),
            pltpu.VMEM((D_MODEL, 2 * TF), BF16),
            pltpu.VMEM((TF, D_MODEL), BF16),
            pltpu.VMEM((TM_MOE, D_MODEL), F32),
            pltpu.SemaphoreType.DMA(()),
            pltpu.SemaphoreType.DMA(()),
        ],
    )
    return pl.pallas_call(
        _moe_kernel,
        out_shape=jax.ShapeDtypeStruct((ROWS_SORTED, D_MODEL), F32),
        grid_spec=grid_spec,
        compiler_params=_cparams(2),
        name="moe_experts",
    )(ce, cs, cn, nch, xs, w1, b1.reshape(DEPTH, N_EXPERTS, 1, 2 * D_FF), w2,
      b2.reshape(DEPTH, N_EXPERTS, 1, D_MODEL), jnp.asarray(sel, BF16))


def _combine_kernel(pos_ref, outs_ref, gw_ref, x1_ref, gf_ref, lnw_ref, lnb_ref, o_ref, rows, sem):
    def start(t, carry):
        for kk in range(TOP_K):
            pltpu.make_async_copy(outs_ref.at[pl.ds(pos_ref[0, t * TOP_K + kk], 1), :],
                                  rows.at[kk, pl.ds(t, 1), :], sem).start()
        return carry

    def wait(t, carry):
        for kk in range(TOP_K):
            pltpu.make_async_copy(outs_ref.at[pl.ds(0, 1), :], rows.at[kk, pl.ds(0, 1), :], sem).wait()
        return carry

    lax.fori_loop(0, TM_COMB, start, 0)
    lax.fori_loop(0, TM_COMB, wait, 0)
    gw = gw_ref[...]
    y = gw[:, 0:1] * rows[0]
    for kk in range(1, TOP_K):
        y = y + gw[:, kk:kk + 1] * rows[kk]
    o_ref[...] = _layer_norm(ALPHA * x1_ref[...] + gf_ref[...] * y, lnw_ref[...], lnb_ref[...])


def _combine(layer, outs, pos, gw, x1, mods, ln_w, ln_b):
    nblk = T_ALL // TM_COMB
    return pl.pallas_call(
        _combine_kernel,
        out_shape=jax.ShapeDtypeStruct((T_ALL, D_MODEL), F32),
        grid=(nblk,),
        in_specs=[
            pl.BlockSpec((None, 1, TM_COMB * TOP_K), lambda i: (i, 0, 0), memory_space=pltpu.SMEM),
            pl.BlockSpec(memory_space=pl.ANY),
            pl.BlockSpec((TM_COMB, LANES), lambda i: (i, 0)),
            pl.BlockSpec((TM_COMB, D_MODEL), lambda i: (i, 0)),
            _mod_spec(layer, 5, TM_COMB),
            pl.BlockSpec((None, None, 1, D_MODEL), lambda i: (layer, 1, 0, 0)),
            pl.BlockSpec((None, None, 1, D_MODEL), lambda i: (layer, 1, 0, 0)),
        ],
        out_specs=pl.BlockSpec((TM_COMB, D_MODEL), lambda i: (i, 0)),
        scratch_shapes=[pltpu.VMEM((TOP_K, TM_COMB, D_MODEL), F32), pltpu.SemaphoreType.DMA(())],
        compiler_params=_cparams(1),
        name="moe_combine",
    )(pos.reshape(nblk, 1, TM_COMB * TOP_K), outs, gw, x1, mods,
      ln_w.reshape(DEPTH, 2, 1, D_MODEL), ln_b.reshape(DEPTH, 2, 1, D_MODEL))


def _routing_tables(top_i):
    flat_e = top_i.reshape(-1)
    onehot = (flat_e[:, None] == jnp.arange(N_EXPERTS, dtype=jnp.int32)[None, :]).astype(jnp.int32)
    rank = jnp.take_along_axis(jnp.cumsum(onehot, axis=0), flat_e[:, None], axis=1)[:, 0] - 1
    counts = jnp.sum(onehot, axis=0)
    ntiles = (counts + TM_MOE - 1) // TM_MOE
    tile_end = jnp.cumsum(ntiles)
    tile_start = tile_end - ntiles
    dest = tile_start[flat_e] * TM_MOE + rank
    tok_sorted = jnp.zeros((ROWS_SORTED,), jnp.int32).at[dest].set(
        jnp.arange(N_ASSIGN, dtype=jnp.int32) // TOP_K)
    nchunks = (ntiles + TPC - 1) // TPC
    chunk_end = jnp.cumsum(nchunks)
    total = chunk_end[-1]
    cidx = jnp.arange(NC_MAX, dtype=jnp.int32)
    last = jnp.maximum(total - 1, 0)
    ce_raw = jnp.searchsorted(chunk_end, jnp.minimum(cidx, last), side="right").astype(jnp.int32)
    ce = jnp.clip(ce_raw, 0, N_EXPERTS - 1)
    kk = cidx - (chunk_end - nchunks)[ce]
    cs = tile_start[ce] + kk * TPC
    cn = jnp.where(cidx < total, jnp.clip(ntiles[ce] - kk * TPC, 0, TPC), 0)
    cs = jnp.where(cidx < total, cs, 0)
    tables = (ce.astype(jnp.int32), cs.astype(jnp.int32), cn.astype(jnp.int32),
              jnp.stack([total, tile_end[-1]]).astype(jnp.int32))
    return tok_sorted, dest.astype(jnp.int32), tables


def kernel(x_prompt, x_sample, c, cache_na_k, cache_na_v, state_mlstm_c, state_mlstm_n, state_mlstm_m,
           c_ctx, w_mod, b_mod, w_in, b_gate, mlstm_norm_w, na_rpb, w_out, ln_w, ln_b,
           router_w, router_b, expert_w1, expert_b1, expert_w2, expert_b2):
    g0 = 4 * W_M
    w_main = jnp.concatenate([w_in[:, :, :g0], w_in[:, :, g0 + N_GATES:]], axis=-1).astype(BF16)
    wg = w_in[:, :, g0:g0 + N_GATES].reshape(DEPTH, D_MODEL, 4, H_M).transpose(0, 1, 3, 2)
    w_gate = jnp.pad(wg, ((0, 0), (0, 0), (0, 0), (0, LANES - 4))).reshape(DEPTH, D_MODEL, GATE_W).astype(BF16)
    bg = b_gate.reshape(DEPTH, 4, H_M).transpose(0, 2, 1)
    bg = jnp.pad(bg, ((0, 0), (0, 0), (0, LANES - 4))).reshape(DEPTH, 1, GATE_W)
    w_out_b = w_out.astype(BF16)
    rw = jnp.pad(router_w, ((0, 0), (0, 0), (0, LANES - N_EXPERTS)))
    rb = jnp.pad(router_b, ((0, 0), (0, LANES - N_EXPERTS)), constant_values=NEG).reshape(DEPTH, 1, LANES)

    cvec = jnp.concatenate([c_ctx[None, :], c, jnp.zeros((MOD_ROWS - 1 - DEC_BATCH, D_MODEL), F32)], axis=0)
    mods = _adaln(cvec, w_mod, b_mod).reshape(DEPTH * MOD_ROWS * 6, 1, D_MODEL)

    x = jnp.concatenate([x_prompt.reshape(T_CTX, D_MODEL), x_sample.reshape(T_LAT, D_MODEL)], axis=0)
    ks_l, vs_l, cs_l, ns_l, ms_l = [], [], [], [], []
    for l in range(DEPTH):
        proj, gates = _in_proj(l, x, mods, w_main, w_gate, bg)
        m_ctx, st_c, st_n, st_m = _mlstm(l, proj, gates, mlstm_norm_w, latent=False)
        (m_lat,) = _mlstm(l, proj, gates, mlstm_norm_w, latent=True,
                          state=(state_mlstm_c, state_mlstm_n, state_mlstm_m))
        a_ctx = _attention(l, proj, latent=False)
        a_lat = _attention(l, proj, latent=True, cache_k=cache_na_k, cache_v=cache_na_v, rpb=na_rpb)
        x1, h2, top_i, top_g = _out_proj(l, m_ctx, m_lat, a_ctx, a_lat, w_out_b, x, mods, ln_w, ln_b, rw, rb)
        tok_sorted, pos, tables = _routing_tables(top_i[:, :TOP_K])
        xs = _gather_rows(h2, tok_sorted)
        outs = _moe_experts(l, xs, tables, expert_w1, expert_b1, expert_w2, expert_b2)
        x = _combine(l, outs, pos, top_g, x1, mods, ln_w, ln_b)

        k0 = 4 * W_M + W_NA
        ks_l.append(proj[:T_CTX, k0:k0 + W_NA].reshape(BATCH, SEQ, H_NA, D_NA))
        vs_l.append(proj[:T_CTX, k0 + W_NA:k0 + 2 * W_NA].reshape(BATCH, SEQ, H_NA, D_NA))
        cs_l.append(st_c)
        ns_l.append(st_n.reshape(BATCH, 2, H_M, DK_M))
        ms_l.append(st_m[:, :, :, 0, 0])

    y_prompt = x[:T_CTX].reshape(BATCH, SEQ, D_MODEL)
    y_sample = x[T_CTX:].reshape(DEC_BATCH, DEC_SEQ, D_MODEL)
    return (y_prompt, y_sample, jnp.stack(ks_l, axis=1), jnp.stack(vs_l, axis=1),
            jnp.stack(cs_l, axis=1), jnp.stack(ns_l, axis=1), jnp.stack(ms_l, axis=1))
```

```python
import functools

import numpy as np
import jax
import jax.numpy as jnp
from jax import lax
from jax.experimental import pallas as pl
from jax.experimental.pallas import tpu as pltpu

F32 = jnp.float32
BF16 = jnp.bfloat16

D_MODEL = 2048
BATCH, SEQ = 16, 256
DEC_BATCH, DEC_SEQ = 8, 1024
DEPTH = 2
PAST_LEN = 512
GRID_W = 64
H_M, DK_M, DV_M = 4, 256, 256
H_NA, D_NA = 8, 128
MAX_KH, KW = 8, 16
N_EXPERTS, TOP_K, D_FF = 32, 4, 2048
SWIGLU_LIMIT, SWIGLU_ALPHA = 7.0, 1.702
ROPE_BASE = 10000.0
CHUNK = 128
N_GATES = 4 * H_M
W_M = H_M * DV_M
W_NA = H_NA * D_NA
ALPHA = (2 * DEPTH) ** 0.25
NEG = -1e30
LN_EPS = 1e-5
HN_EPS = 1e-6

T_CTX = BATCH * SEQ
T_LAT = DEC_BATCH * DEC_SEQ
T_ALL = T_CTX + T_LAT
N_MAIN = 4 * W_M + 3 * W_NA
N_ASSIGN = T_ALL * TOP_K

LANES = 128
VMEM_LIMIT = 56 * 1024 * 1024

TM_IN = 512
TN_IN = 1024
TM_OUT = 256
TM_ROUTE = 256
TM_MOE = 256
CH_MOE = 2048
TPC = CH_MOE // TM_MOE
TF = 256
NJ = D_FF // TF
ROWS_SORTED = N_ASSIGN + N_EXPERTS * TM_MOE
NT_SORTED = ROWS_SORTED // TM_MOE
NC_MAX = N_EXPERTS + -(-NT_SORTED // TPC)
TM_COMB = 128
D_PACK = D_MODEL // 2
RANK_BITS = 16
assert N_ASSIGN < (1 << RANK_BITS)


def _cparams(n_axes, vmem=VMEM_LIMIT):
    return pltpu.CompilerParams(dimension_semantics=("arbitrary",) * n_axes, vmem_limit_bytes=vmem)


def _dot(a, b):
    return jnp.dot(a, b, preferred_element_type=F32)


def _split2(x):
    hi = x.astype(BF16)
    lo = (x - hi.astype(F32)).astype(BF16)
    return hi, lo


def _dot_f32(a, b):
    a1, a2 = _split2(a)
    b1, b2 = _split2(b)
    return _dot(a1, b1) + (_dot(a1, b2) + _dot(a2, b1))


def _layer_norm(z, w, b):
    mu = jnp.mean(z, axis=-1, keepdims=True)
    zc = z - mu
    var = jnp.mean(zc * zc, axis=-1, keepdims=True)
    return zc * lax.rsqrt(var + LN_EPS) * w + b


MOD_ROWS = 16
TN_MOD = 512


def _adaln_kernel(c_ref, w_ref, b_ref, o_ref):
    c = c_ref[...]
    s = c * jax.nn.sigmoid(c)
    o_ref[...] = _dot_f32(s, w_ref[...]) + b_ref[...]


def _adaln(cvec, w_mod, b_mod):
    n = 6 * D_MODEL
    return pl.pallas_call(
        _adaln_kernel,
        out_shape=jax.ShapeDtypeStruct((DEPTH, MOD_ROWS, n), F32),
        grid=(DEPTH, n // TN_MOD),
        in_specs=[
            pl.BlockSpec((MOD_ROWS, D_MODEL), lambda l, j: (0, 0)),
            pl.BlockSpec((None, D_MODEL, TN_MOD), lambda l, j: (l, 0, j)),
            pl.BlockSpec((None, 1, TN_MOD), lambda l, j: (l, 0, j)),
        ],
        out_specs=pl.BlockSpec((None, MOD_ROWS, TN_MOD), lambda l, j: (l, 0, j)),
        compiler_params=_cparams(2),
        name="adaln",
    )(cvec, w_mod, b_mod.reshape(DEPTH, 1, n))


def _mod_spec(layer, k, rows_per_tile):
    ctx_tiles = T_CTX // rows_per_tile
    tiles_per_lat = DEC_SEQ // rows_per_tile

    def index(i, *_):
        r = jnp.where(i < ctx_tiles, 0, 1 + (i - ctx_tiles) // tiles_per_lat)
        return ((layer * MOD_ROWS + r) * 6 + k, 0, 0)

    return pl.BlockSpec((None, 1, D_MODEL), index)


GATE_W = H_M * LANES


CTX_TILES_IN = T_CTX // TM_IN
SEQ_PER_TILE = TM_IN // SEQ
J_NK = (4 * W_M + W_NA) // TN_IN
J_NV = (4 * W_M + 2 * W_NA) // TN_IN
KV_TILES = W_NA // TN_IN


def _inproj_kernel(x_ref, sh_ref, sc_ref, w_ref, wg_ref, bg_ref, o_ref, g_ref, ko_ref, vo_ref, h_scr):
    i = pl.program_id(0)
    j = pl.program_id(1)

    @pl.when(j == 0)
    def _():
        h = (x_ref[...] * (1.0 + sc_ref[...]) + sh_ref[...]).astype(BF16)
        h_scr[...] = h
        g_ref[...] = _dot(h, wg_ref[...]) + bg_ref[...]

    out = _dot(h_scr[...], w_ref[...])
    o_ref[...] = out
    is_ctx = i < CTX_TILES_IN

    @pl.when(is_ctx & (j >= J_NK) & (j < J_NK + KV_TILES))
    def _():
        ko_ref[...] = out.reshape(SEQ_PER_TILE, SEQ, TN_IN)

    @pl.when(is_ctx & (j >= J_NV))
    def _():
        vo_ref[...] = out.reshape(SEQ_PER_TILE, SEQ, TN_IN)


def _kv_spec(j_first):
    def index(i, j):
        jt = jnp.clip(j - j_first, 0, KV_TILES - 1)
        ctx = i < CTX_TILES_IN
        return (jnp.where(ctx, i, CTX_TILES_IN - 1), 0, jnp.where(ctx, jt, KV_TILES - 1))

    return pl.BlockSpec((SEQ_PER_TILE, SEQ, TN_IN), index)


def _in_proj(layer, x, mods, w_main, w_gate, b_gate):
    kv_shape = jax.ShapeDtypeStruct((BATCH, SEQ, W_NA), F32)
    return pl.pallas_call(
        _inproj_kernel,
        out_shape=(jax.ShapeDtypeStruct((T_ALL, N_MAIN), F32),
                   jax.ShapeDtypeStruct((T_ALL, GATE_W), F32), kv_shape, kv_shape),
        grid=(T_ALL // TM_IN, N_MAIN // TN_IN),
        in_specs=[
            pl.BlockSpec((TM_IN, D_MODEL), lambda i, j: (i, 0)),
            _mod_spec(layer, 0, TM_IN),
            _mod_spec(layer, 1, TM_IN),
            pl.BlockSpec((None, D_MODEL, TN_IN), lambda i, j: (layer, 0, j)),
            pl.BlockSpec((None, D_MODEL, GATE_W), lambda i, j: (layer, 0, 0)),
            pl.BlockSpec((None, 1, GATE_W), lambda i, j: (layer, 0, 0)),
        ],
        out_specs=(pl.BlockSpec((TM_IN, TN_IN), lambda i, j: (i, j)),
                   pl.BlockSpec((TM_IN, GATE_W), lambda i, j: (i, 0)),
                   _kv_spec(J_NK), _kv_spec(J_NV)),
        scratch_shapes=[pltpu.VMEM((TM_IN, D_MODEL), BF16)],
        compiler_params=_cparams(2),
        name="in_proj",
    )(x, mods, mods, w_main, w_gate, b_gate)


def _log_sigmoid(x):
    return jnp.minimum(x, 0.0) - jnp.log(1.0 + jnp.exp(-jnp.abs(x)))


def _mlstm_kernel(*refs, seq, use_rope, has_init, emit_state, layer):
    it = iter(refs)
    q_ref, k_ref, v_ref, mo_ref, g_ref, nw_ref = (next(it) for _ in range(6))
    if use_rope:
        cos_ref, sin_ref = next(it), next(it)
    if has_init:
        c0_ref, n0_ref, m0_ref = next(it), next(it), next(it)
    out_ref = next(it)
    if emit_state:
        co_ref, no_ref, mso_ref = next(it), next(it), next(it)
    qs, ks, hs, c_scr = next(it), next(it), next(it), next(it)

    nc = seq // CHUNK
    b_idx = pl.program_id(0)
    h_idx = pl.program_id(1)

    def prep(j, carry):
        r0 = pl.multiple_of(j * CHUNK, CHUNK)
        rows = pl.ds(r0, CHUNK)
        q = q_ref[rows, :]
        k = k_ref[rows, :]
        if use_rope:
            cos = cos_ref[rows, :]
            sin = sin_ref[rows, :]

            def rope(x):
                halves = []
                for p in range(2):
                    xp = x[:, p * LANES:(p + 1) * LANES]
                    halves.append(xp * cos[:, p * LANES:(p + 1) * LANES]
                                  + pltpu.roll(xp, LANES // 2, axis=1) * sin[:, p * LANES:(p + 1) * LANES])
                return jnp.concatenate(halves, axis=1)

            q = rope(q)
            k = rope(k)
        qs[rows, :] = (q * (DK_M ** -0.5)).astype(BF16)
        ks[rows, :] = k.astype(BF16)
        return carry

    lax.fori_loop(0, nc, prep, 0)

    row_i = lax.broadcasted_iota(jnp.int32, (CHUNK, CHUNK), 0)
    col_i = lax.broadcasted_iota(jnp.int32, (CHUNK, CHUNK), 1)

    def chunk_rows(j):
        return pl.ds(pl.multiple_of(j * CHUNK, CHUNK), CHUNK)

    init = []
    for d in range(2):
        if has_init:
            c_scr[d] = c0_ref[d]
            init += [n0_ref[d], jnp.full((1, 1), m0_ref[((b_idx * DEPTH + layer) * 2 + d) * H_M + h_idx], F32)]
        else:
            c_scr[d] = jnp.zeros((DK_M, DV_M), F32)
            init += [jnp.zeros((1, DK_M), F32), jnp.full((1, 1), NEG, F32)]

    def chunk_step(d, j, n, m):
        mask = (col_i <= row_i) if d == 0 else (col_i >= row_i)
        tri = mask.astype(BF16)
        lane_i, lane_b = 2 * d, 2 * d + 1
        end_row = CHUNK - 1 if d == 0 else 0
        rows = chunk_rows(j)
        gates = g_ref[rows, :]
        lf = _log_sigmoid(gates)
        lf1, lf2 = _split2(lf)
        lf3 = (lf - lf1.astype(F32) - lf2.astype(F32)).astype(BF16)
        bc = _dot(tri, lf1) + (_dot(tri, lf2) + _dot(tri, lf3))
        gates_t = gates.T
        bc_t = bc.T
        b_col = bc[:, lane_b:lane_b + 1]
        ig_col = gates[:, lane_i:lane_i + 1]
        b_row = bc_t[lane_b:lane_b + 1, :]
        ig_row = gates_t[lane_i:lane_i + 1, :]
        b_end = bc[end_row:end_row + 1, lane_b:lane_b + 1]

        dlog = jnp.where(mask, (b_col - b_row) + ig_row, NEG)
        g = b_col + m
        mt = jnp.maximum(g, jnp.max(dlog, axis=1, keepdims=True))
        qc = qs[rows, :]
        kc = ks[rows, :]
        vc = v_ref[rows, :].astype(BF16)
        s = lax.dot_general(qc, kc, (((1,), (1,)), ((), ())), preferred_element_type=F32)
        s = s * jnp.exp(dlog - mt)
        sg = jnp.exp(g - mt)
        c_old = c_scr[d]
        num = _dot(s.astype(BF16), vc) + sg * _dot(qc, c_old.astype(BF16))
        den = (jnp.sum(s, axis=1, keepdims=True)
               + sg * jnp.sum(qc.astype(F32) * n, axis=1, keepdims=True))
        h = num / jnp.maximum(jnp.abs(den), jnp.exp(-mt))

        wlog = (b_end - b_col) + ig_col
        m_new = jnp.maximum(b_end + m, jnp.max(wlog, axis=0, keepdims=True))
        ws = jnp.exp(wlog - m_new)
        sc = jnp.exp(b_end + m - m_new)
        kw = kc.astype(F32) * ws
        c_scr[d] = sc * c_old + _dot(kw.T.astype(BF16), vc)
        n_new = sc * n + jnp.sum(kw, axis=0, keepdims=True)
        return h, n_new, m_new

    def finish(j, ht):
        rows = chunk_rows(j)
        mu = jnp.mean(ht, axis=1, keepdims=True)
        hc = ht - mu
        var = jnp.mean(hc * hc, axis=1, keepdims=True)
        hn = hc * lax.rsqrt(var + HN_EPS) * nw_ref[...]
        out_ref[rows, :] = (jax.nn.sigmoid(mo_ref[rows, :]) * hn).astype(out_ref.dtype)

    def first_half(jj, carry):
        nf, mf, nb, mb = carry
        jb = nc - 1 - jj
        hf, nf, mf = chunk_step(0, jj, nf, mf)
        hb, nb, mb = chunk_step(1, jb, nb, mb)
        hs[chunk_rows(jj), :] = hf
        hs[chunk_rows(jb), :] = hb
        return nf, mf, nb, mb

    def second_half(jj, carry):
        nf, mf, nb, mb = carry
        jb = nc - 1 - jj
        hf, nf, mf = chunk_step(0, jj, nf, mf)
        hb, nb, mb = chunk_step(1, jb, nb, mb)
        finish(jj, hs[chunk_rows(jj), :] + hf)
        finish(jb, hs[chunk_rows(jb), :] + hb)
        return nf, mf, nb, mb

    carry = lax.fori_loop(0, nc // 2, first_half, tuple(init))
    n_f, m_f, n_b, m_b = lax.fori_loop(nc // 2, nc, second_half, carry)
    if emit_state:
        for d, (n_fin, m_fin) in enumerate(((n_f, m_f), (n_b, m_b))):
            co_ref[d] = c_scr[d]
            no_ref[d] = n_fin
            mso_ref[d] = jnp.broadcast_to(m_fin, (1, LANES))


def _rope_tables(seq):
    pos = np.arange(seq)
    half = DK_M // 2
    nf = half // 2
    inv = ROPE_BASE ** (-np.arange(nf, dtype=np.float32) / nf)
    cos_parts, sin_parts = [], []
    for p in ((pos // GRID_W).astype(np.float32), (pos % GRID_W).astype(np.float32)):
        ang = (p[:, None] * inv).astype(np.float32)
        cos_parts += [np.cos(ang), np.cos(ang)]
        sin_parts += [-np.sin(ang), np.sin(ang)]
    return (jnp.asarray(np.concatenate(cos_parts, axis=1), F32),
            jnp.asarray(np.concatenate(sin_parts, axis=1), F32))


def _mlstm(layer, proj, gates, norm_w, *, latent, state=None):
    if latent:
        nb, seq, rb0 = DEC_BATCH, DEC_SEQ, T_CTX // DEC_SEQ
    else:
        nb, seq, rb0 = BATCH, SEQ, 0
    cb = W_M // DK_M
    in_specs = [
        pl.BlockSpec((seq, DK_M), lambda b, h: (rb0 + b, h)),
        pl.BlockSpec((seq, DK_M), lambda b, h: (rb0 + b, cb + h)),
        pl.BlockSpec((seq, DV_M), lambda b, h: (rb0 + b, 2 * cb + h)),
        pl.BlockSpec((seq, DV_M), lambda b, h: (rb0 + b, 3 * cb + h)),
        pl.BlockSpec((seq, LANES), lambda b, h: (rb0 + b, h)),
        pl.BlockSpec((None, 1, DV_M), lambda b, h: (layer, 0, h)),
    ]
    args = [proj, proj, proj, proj, gates, norm_w.reshape(DEPTH, 1, W_M)]
    if latent:
        cos, sin = _rope_tables(seq)
        in_specs += [pl.BlockSpec((seq, DK_M), lambda b, h: (0, 0))] * 2
        args += [cos, sin]
        st_c, st_n, st_m = state
        in_specs += [
            pl.BlockSpec((None, None, 2, None, DK_M, DV_M), lambda b, h: (b, layer, 0, h, 0, 0)),
            pl.BlockSpec((None, None, 2, None, 1, DK_M), lambda b, h: (b, layer, 0, h, 0, 0)),
            pl.BlockSpec(memory_space=pltpu.SMEM),
        ]
        args += [st_c, st_n.reshape(DEC_BATCH, DEPTH, 2, H_M, 1, DK_M), st_m.reshape(-1)]
    out_shape = [jax.ShapeDtypeStruct((nb * seq, W_M), BF16)]
    out_specs = [pl.BlockSpec((seq, DV_M), lambda b, h: (b, h))]
    if not latent:
        out_shape += [
            jax.ShapeDtypeStruct((nb, 2, H_M, DK_M, DV_M), F32),
            jax.ShapeDtypeStruct((nb, 2, H_M, 1, DK_M), F32),
            jax.ShapeDtypeStruct((nb, 2, H_M, 1, LANES), F32),
        ]
        out_specs += [
            pl.BlockSpec((None, 2, None, DK_M, DV_M), lambda b, h: (b, 0, h, 0, 0)),
            pl.BlockSpec((None, 2, None, 1, DK_M), lambda b, h: (b, 0, h, 0, 0)),
            pl.BlockSpec((None, 2, None, 1, LANES), lambda b, h: (b, 0, h, 0, 0)),
        ]
    kern = functools.partial(_mlstm_kernel, seq=seq, use_rope=latent, has_init=latent,
                             emit_state=not latent, layer=layer)
    return pl.pallas_call(
        kern,
        out_shape=tuple(out_shape),
        grid=(nb, H_M),
        in_specs=in_specs,
        out_specs=tuple(out_specs),
        scratch_shapes=[
            pltpu.VMEM((seq, DK_M), BF16),
            pltpu.VMEM((seq, DK_M), BF16),
            pltpu.VMEM((seq, DV_M), F32),
            pltpu.VMEM((2, DK_M, DV_M), F32),
        ],
        compiler_params=_cparams(2),
        name="mlstm_lat" if latent else "mlstm_ctx",
    )(*args)


RPB_ROWS = 2 * MAX_KH - 1
RPB_COLS = 2 * KW - 1
Q_TILE = 256
LAT_ROWS = DEC_SEQ // GRID_W
KH = min(MAX_KH, LAT_ROWS)


def _attn_ctx_kernel(q_ref, k_ref, v_ref, o_ref):
    scale = D_NA ** -0.5
    q = q_ref[...].astype(BF16)
    k = k_ref[...].astype(BF16)
    v = v_ref[...].astype(BF16)
    s = lax.dot_general(q, k, (((1,), (1,)), ((), ())), preferred_element_type=F32) * scale
    m = jnp.max(s, axis=1, keepdims=True)
    p = jnp.exp(s - m)
    l = jnp.sum(p, axis=1, keepdims=True)
    o_ref[...] = (_dot(p.astype(BF16), v) / l).astype(o_ref.dtype)


def _attn_lat_kernel(rpb_ref, q_ref, k_ref, v_ref, kc_ref, vc_ref, o_ref, bias_scr, g_scr):
    h_idx = pl.program_id(0)
    scale = D_NA ** -0.5

    @pl.when(pl.program_id(1) == 0)
    def _build_bias():
        qc = lax.broadcasted_iota(jnp.int32, (GRID_W, LANES), 0)
        lane = lax.broadcasted_iota(jnp.int32, (GRID_W, LANES), 1)
        kc = lane & (GRID_W - 1)
        upper = lane >= GRID_W
        cidx = jnp.clip(kc - qc, -(KW - 1), KW - 1) + (KW - 1)
        cs = jnp.clip(qc - KW // 2, 0, GRID_W - KW)
        col_ok = (kc >= cs) & (kc < cs + KW)
        base = h_idx * (RPB_ROWS * RPB_COLS)
        for dr0 in range(-KH, KH):
            acc = jnp.zeros((GRID_W, LANES), F32)
            for c in range(RPB_COLS):
                lo = rpb_ref[base + (dr0 + MAX_KH - 1) * RPB_COLS + c] if -MAX_KH < dr0 < MAX_KH else 0.0
                hi = rpb_ref[base + (dr0 + MAX_KH) * RPB_COLS + c] if -MAX_KH < dr0 + 1 < MAX_KH else 0.0
                acc = jnp.where(cidx == c, jnp.where(upper, hi, lo), acc)
            g_scr[dr0 + KH] = jnp.where(col_ok, acc, NEG)
        neg_tile = jnp.full((GRID_W, LANES), NEG, F32)
        for qr in range(LAT_ROWS):
            rs = min(max(qr - KH // 2, 0), LAT_ROWS - KH)
            for t in range(LAT_ROWS // 2):
                ok0 = rs <= 2 * t < rs + KH
                ok1 = rs <= 2 * t + 1 < rs + KH
                if ok0 or ok1:
                    tile = g_scr[2 * t - qr + KH]
                    if not ok1:
                        tile = jnp.where(upper, NEG, tile)
                    if not ok0:
                        tile = jnp.where(upper, tile, NEG)
                else:
                    tile = neg_tile
                bias_scr[qr * GRID_W:(qr + 1) * GRID_W, t * LANES:(t + 1) * LANES] = tile

    k = k_ref[...].astype(BF16)
    v = v_ref[...].astype(BF16)
    kc = kc_ref[...].astype(BF16)
    vc = vc_ref[...].astype(BF16)
    dn = (((1,), (1,)), ((), ()))
    for qt in range(DEC_SEQ // Q_TILE):
        rows = slice(qt * Q_TILE, (qt + 1) * Q_TILE)
        q = q_ref[rows, :].astype(BF16)
        s_loc = lax.dot_general(q, k, dn, preferred_element_type=F32) * scale + bias_scr[rows, :]
        s_ctx = lax.dot_general(q, kc, dn, preferred_element_type=F32) * scale
        m = jnp.maximum(jnp.max(s_loc, axis=1, keepdims=True), jnp.max(s_ctx, axis=1, keepdims=True))
        p_loc = jnp.exp(s_loc - m)
        p_ctx = jnp.exp(s_ctx - m)
        l = jnp.sum(p_loc, axis=1, keepdims=True) + jnp.sum(p_ctx, axis=1, keepdims=True)
        o = _dot(p_loc.astype(BF16), v) + _dot(p_ctx.astype(BF16), vc)
        o_ref[rows, :] = (o / l).astype(o_ref.dtype)


def _attention(layer, proj, *, latent, cache_k=None, cache_v=None, rpb=None):
    qb0 = 4 * W_M // D_NA
    if not latent:
        return pl.pallas_call(
            _attn_ctx_kernel,
            out_shape=jax.ShapeDtypeStruct((T_CTX, W_NA), BF16),
            grid=(BATCH, H_NA),
            in_specs=[
                pl.BlockSpec((SEQ, D_NA), lambda b, h: (b, qb0 + h)),
                pl.BlockSpec((SEQ, D_NA), lambda b, h: (b, qb0 + H_NA + h)),
                pl.BlockSpec((SEQ, D_NA), lambda b, h: (b, qb0 + 2 * H_NA + h)),
            ],
            out_specs=pl.BlockSpec((SEQ, D_NA), lambda b, h: (b, h)),
            compiler_params=_cparams(2),
            name="attn_ctx",
        )(proj, proj, proj)
    rb0 = T_CTX // DEC_SEQ
    ck = cache_k.reshape(DEC_BATCH, DEPTH, PAST_LEN, W_NA)
    cv = cache_v.reshape(DEC_BATCH, DEPTH, PAST_LEN, W_NA)
    grid_spec = pltpu.PrefetchScalarGridSpec(
        num_scalar_prefetch=1,
        grid=(H_NA, DEC_BATCH),
        in_specs=[
            pl.BlockSpec((DEC_SEQ, D_NA), lambda h, b, r: (rb0 + b, qb0 + h)),
            pl.BlockSpec((DEC_SEQ, D_NA), lambda h, b, r: (rb0 + b, qb0 + H_NA + h)),
            pl.BlockSpec((DEC_SEQ, D_NA), lambda h, b, r: (rb0 + b, qb0 + 2 * H_NA + h)),
            pl.BlockSpec((None, None, PAST_LEN, D_NA), lambda h, b, r: (b, layer, 0, h)),
            pl.BlockSpec((None, None, PAST_LEN, D_NA), lambda h, b, r: (b, layer, 0, h)),
        ],
        out_specs=pl.BlockSpec((DEC_SEQ, D_NA), lambda h, b, r: (b, h)),
        scratch_shapes=[
            pltpu.VMEM((DEC_SEQ, DEC_SEQ), F32),
            pltpu.VMEM((2 * KH, GRID_W, LANES), F32),
        ],
    )
    return pl.pallas_call(
        _attn_lat_kernel,
        out_shape=jax.ShapeDtypeStruct((T_LAT, W_NA), BF16),
        grid_spec=grid_spec,
        compiler_params=_cparams(2),
        name="attn_lat",
    )(rpb[layer].reshape(-1), proj, proj, proj, ck, cv)


def _outproj_kernel(mc_ref, ml_ref, ac_ref, al_ref, w_ref, x_ref, ga_ref, shf_ref, scf_ref,
                    lnw_ref, lnb_ref, rw_ref, rb_ref, x1_ref, h2_ref, ti_ref, tg_ref):
    is_ctx = pl.program_id(0) < T_CTX // TM_OUT
    m = jnp.where(is_ctx, mc_ref[...], ml_ref[...])
    a = jnp.where(is_ctx, ac_ref[...], al_ref[...])
    y = _dot(m, w_ref[0:W_M, :]) + _dot(a, w_ref[W_M:W_M + W_NA, :])
    x1 = _layer_norm(ALPHA * x_ref[...] + ga_ref[...] * y, lnw_ref[...], lnb_ref[...])
    x1_ref[...] = x1
    h2 = x1 * (1.0 + scf_ref[...]) + shf_ref[...]
    bits = lax.bitcast_convert_type(h2.astype(BF16).astype(F32), jnp.uint32)
    h2_ref[...] = (bits[:, :D_PACK] >> 16) | (bits[:, D_PACK:] & jnp.uint32(0xFFFF0000))
    logits = _dot_f32(h2, rw_ref[...]) + rb_ref[...]
    lane = lax.broadcasted_iota(jnp.int32, logits.shape, 1)
    vals = logits
    top_v, top_i = [], []
    for _ in range(TOP_K):
        mx = jnp.max(vals, axis=1, keepdims=True)
        ix = jnp.min(jnp.where(vals == mx, lane, LANES), axis=1, keepdims=True)
        top_v.append(mx)
        top_i.append(ix)
        vals = jnp.where(lane == ix, -jnp.inf, vals)
    ex = [jnp.exp(v - top_v[0]) for v in top_v]
    den = ex[0] + ex[1] + ex[2] + ex[3]
    ti = jnp.zeros(logits.shape, jnp.int32)
    tg = jnp.zeros(logits.shape, F32)
    for kk in range(TOP_K):
        ti = jnp.where(lane == kk, top_i[kk], ti)
        tg = jnp.where(lane == kk, ex[kk] / den, tg)
    ti_ref[...] = ti
    tg_ref[...] = tg


def _out_proj(layer, m_ctx, m_lat, a_ctx, a_lat, w_out, x, mods, ln_w, ln_b, router_w, router_b):
    nctx = T_CTX // TM_OUT
    ctx_spec = lambda w: pl.BlockSpec((TM_OUT, w), lambda i: (jnp.minimum(i, nctx - 1), 0))
    lat_spec = lambda w: pl.BlockSpec((TM_OUT, w), lambda i: (jnp.maximum(i - nctx, 0), 0))
    row_spec = lambda w: pl.BlockSpec((TM_OUT, w), lambda i: (i, 0))
    return pl.pallas_call(
        _outproj_kernel,
        out_shape=(jax.ShapeDtypeStruct((T_ALL, D_MODEL), F32),
                   jax.ShapeDtypeStruct((T_ALL, D_PACK), jnp.uint32),
                   jax.ShapeDtypeStruct((T_ALL, LANES), jnp.int32),
                   jax.ShapeDtypeStruct((T_ALL, LANES), F32)),
        grid=(T_ALL // TM_OUT,),
        in_specs=[
            ctx_spec(W_M), lat_spec(W_M), ctx_spec(W_NA), lat_spec(W_NA),
            pl.BlockSpec((None, W_M + W_NA, D_MODEL), lambda i: (layer, 0, 0)),
            row_spec(D_MODEL),
            _mod_spec(layer, 2, TM_OUT), _mod_spec(layer, 3, TM_OUT), _mod_spec(layer, 4, TM_OUT),
            pl.BlockSpec((None, None, 1, D_MODEL), lambda i: (layer, 0, 0, 0)),
            pl.BlockSpec((None, None, 1, D_MODEL), lambda i: (layer, 0, 0, 0)),
            pl.BlockSpec((None, D_MODEL, LANES), lambda i: (layer, 0, 0)),
            pl.BlockSpec((None, 1, LANES), lambda i: (layer, 0, 0)),
        ],
        out_specs=(row_spec(D_MODEL), row_spec(D_PACK), row_spec(LANES), row_spec(LANES)),
        compiler_params=_cparams(1),
        name="out_proj",
    )(m_ctx, m_lat, a_ctx, a_lat, w_out, x, mods, mods, mods,
      ln_w.reshape(DEPTH, 2, 1, D_MODEL), ln_b.reshape(DEPTH, 2, 1, D_MODEL), router_w, router_b)


def _route_kernel(ti_ref, code_ref, cnt_ref, carry):
    @pl.when(pl.program_id(0) == 0)
    def _():
        carry[...] = jnp.zeros((1, LANES), F32)

    ti = ti_ref[...]
    lane = lax.broadcasted_iota(jnp.int32, (TM_ROUTE, LANES), 1)
    hits = [lane == ti[:, kk:kk + 1] for kk in range(TOP_K)]
    cnt = hits[0].astype(F32)
    for kk in range(1, TOP_K):
        cnt = cnt + hits[kk].astype(F32)
    row_i = lax.broadcasted_iota(jnp.int32, (TM_ROUTE, TM_ROUTE), 0)
    col_i = lax.broadcasted_iota(jnp.int32, (TM_ROUTE, TM_ROUTE), 1)
    before = _dot((col_i < row_i).astype(BF16), cnt.astype(BF16)) + carry[...]
    code = jnp.zeros((TM_ROUTE, LANES), jnp.int32)
    for kk in range(TOP_K):
        rank = jnp.sum(jnp.where(hits[kk], before, 0.0), axis=1, keepdims=True).astype(jnp.int32)
        code = jnp.where(lane == kk, (ti[:, kk:kk + 1] << RANK_BITS) | rank, code)
    code_ref[...] = code
    carry[...] = carry[...] + jnp.sum(cnt, axis=0, keepdims=True)
    cnt_ref[...] = carry[...]


def _route(top_i):
    return pl.pallas_call(
        _route_kernel,
        out_shape=(jax.ShapeDtypeStruct((T_ALL, LANES), jnp.int32),
                   jax.ShapeDtypeStruct((1, LANES), F32)),
        grid=(T_ALL // TM_ROUTE,),
        in_specs=[pl.BlockSpec((TM_ROUTE, LANES), lambda i: (i, 0))],
        out_specs=(pl.BlockSpec((TM_ROUTE, LANES), lambda i: (i, 0)),
                   pl.BlockSpec((1, LANES), lambda i: (0, 0))),
        scratch_shapes=[pltpu.VMEM((1, LANES), F32)],
        compiler_params=_cparams(1),
        name="moe_route",
    )(top_i)


def _sorted_row(ts_ref, code):
    return ts_ref[code >> RANK_BITS] * TM_MOE + (code & ((1 << RANK_BITS) - 1))


def _scatter_kernel(ts_ref, te_ref, meta_ref, code_ref, src_ref, xs_ref, zero_scr, sem):
    @pl.when(pl.program_id(0) == 0)
    def _define_padding():
        zero_scr[...] = jnp.zeros((TM_MOE, D_PACK), jnp.uint32)

        def fill(t):
            dst = xs_ref.at[pl.ds(pl.multiple_of(t * TM_MOE, TM_MOE), TM_MOE), :]
            return pltpu.make_async_copy(zero_scr, dst, sem)

        def per_expert(fn):
            def body(e, carry):
                @pl.when(te_ref[e] > ts_ref[e])
                def _():
                    fn(fill(te_ref[e] - 1))
                return carry
            lax.fori_loop(0, N_EXPERTS, body, 0)

        def tail(fn):
            def body(t, carry):
                fn(fill(t))
                return carry
            lax.fori_loop(meta_ref[1], NT_SORTED, body, 0)

        per_expert(lambda cp: cp.start())
        tail(lambda cp: cp.start())
        per_expert(lambda cp: cp.wait())
        tail(lambda cp: cp.wait())

    def start(t, carry):
        for kk in range(TOP_K):
            dst = _sorted_row(ts_ref, code_ref[0, t * TOP_K + kk])
            pltpu.make_async_copy(src_ref.at[pl.ds(t, 1), :], xs_ref.at[pl.ds(dst, 1), :], sem).start(priority=kk % 2)
        return carry

    def wait(t, carry):
        for kk in range(TOP_K):
            pltpu.make_async_copy(src_ref.at[pl.ds(0, 1), :], xs_ref.at[pl.ds(0, 1), :], sem).wait()
        return carry

    lax.fori_loop(0, TM_ROUTE, start, 0)
    lax.fori_loop(0, TM_ROUTE, wait, 0)


def _scatter_rows(src, codes, tile_start, tile_end, meta):
    nblk = T_ALL // TM_ROUTE
    grid_spec = pltpu.PrefetchScalarGridSpec(
        num_scalar_prefetch=3,
        grid=(nblk,),
        in_specs=[
            pl.BlockSpec((None, 1, TM_ROUTE * TOP_K), lambda i, *_: (i, 0, 0), memory_space=pltpu.SMEM),
            pl.BlockSpec((TM_ROUTE, D_PACK), lambda i, *_: (i, 0)),
        ],
        out_specs=pl.BlockSpec(memory_space=pl.ANY),
        scratch_shapes=[pltpu.VMEM((TM_MOE, D_PACK), jnp.uint32), pltpu.SemaphoreType.DMA(())],
    )
    return pl.pallas_call(
        _scatter_kernel,
        out_shape=jax.ShapeDtypeStruct((ROWS_SORTED, D_PACK), jnp.uint32),
        grid_spec=grid_spec,
        compiler_params=_cparams(1),
        name="moe_scatter",
    )(tile_start, tile_end, meta, codes.reshape(nblk, 1, TM_ROUTE * TOP_K), src)


def _moe_kernel(ce_ref, cs_ref, cn_ref, nch_ref, xs_ref, w1_ref, b1_ref, w2_ref, b2_ref, sel_ref,
                out_ref, xbuf, acc, w1b, w2b, stage, sem_in, sem_out):
    c = pl.program_id(0)
    j = pl.program_id(1)
    nt = cn_ref[c]
    row0 = cs_ref[c] * TM_MOE

    def tile_rows(r):
        return pl.ds(pl.multiple_of(r * TM_MOE, TM_MOE), TM_MOE)

    @pl.when(nt > 0)
    def _chunk():
        @pl.when(j == 0)
        def _load_rows():
            def fetch(r, slot):
                src = xs_ref.at[pl.ds(pl.multiple_of(row0 + r * TM_MOE, TM_MOE), TM_MOE), :]
                return pltpu.make_async_copy(src, stage.at[slot], sem_in.at[slot])

            fetch(0, 0).start()

            def load(r, carry):
                slot = r & 1
                fetch(r, slot).wait()

                @pl.when(r + 1 < nt)
                def _():
                    fetch(r + 1, 1 - slot).start()

                words = stage[slot]
                lo = lax.bitcast_convert_type(words << 16, F32)
                hi = lax.bitcast_convert_type(words & jnp.uint32(0xFFFF0000), F32)
                xbuf[tile_rows(r), 0:D_PACK] = lo.astype(BF16)
                xbuf[tile_rows(r), D_PACK:D_MODEL] = hi.astype(BF16)
                acc[tile_rows(r), :] = jnp.broadcast_to(b2_ref[...], (TM_MOE, D_MODEL))
                return carry

            lax.fori_loop(0, nt, load, 0)

        w1b[...] = w1_ref[...].astype(BF16)
        w2b[...] = w2_ref[...].astype(BF16)

        def tile(start, size):
            rows = pl.ds(pl.multiple_of(start, TM_MOE), size)
            hdn = _dot(xbuf[rows, :], w1b[...]) + b1_ref[...]
            glu = jnp.minimum(hdn, SWIGLU_LIMIT)
            lin = jnp.clip(hdn, -SWIGLU_LIMIT, SWIGLU_LIMIT) + 1.0
            z = glu * jax.nn.sigmoid(SWIGLU_ALPHA * glu) * pltpu.roll(lin, 2 * TF - 1, axis=1)
            act = _dot(z.astype(BF16), sel_ref[...]).astype(BF16)
            acc[rows, :] = acc[rows, :] + _dot(act, w2b[...])

        def pair(r, carry):
            tile(r * (2 * TM_MOE), 2 * TM_MOE)
            return carry

        lax.fori_loop(0, nt >> 1, pair, 0)

        @pl.when((nt & 1) == 1)
        def _():
            tile((nt - 1) * TM_MOE, TM_MOE)

        @pl.when(j == NJ - 1)
        def _store_rows():
            def copy(r):
                dst = out_ref.at[pl.ds(pl.multiple_of(row0 + r * TM_MOE, TM_MOE), TM_MOE), :]
                return pltpu.make_async_copy(acc.at[tile_rows(r), :], dst, sem_out)

            def start(r, carry):
                copy(r).start()
                return carry

            def wait(r, carry):
                copy(r).wait()
                return carry

            lax.fori_loop(0, nt, start, 0)
            lax.fori_loop(0, nt, wait, 0)

    @pl.when((c == NC_MAX - 1) & (j == NJ - 1))
    def _zero_unused_tiles():
        zero_rows = acc.at[pl.ds(0, TM_MOE), :]
        zero_rows[...] = jnp.zeros((TM_MOE, D_MODEL), F32)

        def copy(t):
            dst = out_ref.at[pl.ds(pl.multiple_of(t * TM_MOE, TM_MOE), TM_MOE), :]
            return pltpu.make_async_copy(zero_rows, dst, sem_out)

        def start(t, carry):
            copy(t).start()
            return carry

        def wait(t, carry):
            copy(t).wait()
            return carry

        lax.fori_loop(nch_ref[1], NT_SORTED, start, 0)
        lax.fori_loop(nch_ref[1], NT_SORTED, wait, 0)


def _moe_experts(layer, xs, tables, w1, b1, w2, b2):
    ce, cs, cn, nch = tables
    sel = np.zeros((2 * TF, TF), np.float32)
    sel[2 * np.arange(TF), np.arange(TF)] = 1.0

    def jeff(c, j, nch_ref):
        return jnp.where(c < nch_ref[0], j, NJ - 1)

    grid_spec = pltpu.PrefetchScalarGridSpec(
        num_scalar_prefetch=4,
        grid=(NC_MAX, NJ),
        in_specs=[
            pl.BlockSpec(memory_space=pl.ANY),
            pl.BlockSpec((None, None, D_MODEL, 2 * TF),
                         lambda c, j, ce, cs, cn, nch: (layer, ce[c], 0, jeff(c, j, nch))),
            pl.BlockSpec((None, None, 1, 2 * TF),
                         lambda c, j, ce, cs, cn, nch: (layer, ce[c], 0, jeff(c, j, nch))),
            pl.BlockSpec((None, None, TF, D_MODEL),
                         lambda c, j, ce, cs, cn, nch: (layer, ce[c], jeff(c, j, nch), 0)),
            pl.BlockSpec((None, None, 1, D_MODEL),
                         lambda c, j, ce, cs, cn, nch: (layer, ce[c], 0, 0)),
            pl.BlockSpec((2 * TF, TF), lambda c, j, ce, cs, cn, nch: (0, 0)),
        ],
        out_specs=pl.BlockSpec(memory_space=pl.ANY),
        scratch_shapes=[
            pltpu.VMEM((CH_MOE, D_MODEL), BF16),
            pltpu.VMEM((CH_MOE, D_MODEL), F32),
            pltpu.VMEM((D_MODEL, 2 * TF), BF16),
            pltpu.VMEM((TF, D_MODEL), BF16),
            pltpu.VMEM((2, TM_MOE, D_PACK), jnp.uint32),
            pltpu.SemaphoreType.DMA((2,)),
            pltpu.SemaphoreType.DMA(()),
        ],
    )
    return pl.pallas_call(
        _moe_kernel,
        out_shape=jax.ShapeDtypeStruct((ROWS_SORTED, D_MODEL), F32),
        grid_spec=grid_spec,
        compiler_params=_cparams(2),
        name="moe_experts",
    )(ce, cs, cn, nch, xs, w1, b1.reshape(DEPTH, N_EXPERTS, 1, 2 * D_FF), w2,
      b2.reshape(DEPTH, N_EXPERTS, 1, D_MODEL), jnp.asarray(sel, BF16))


SUB_COMB = 16
CTX_TILES_COMB = T_CTX // TM_COMB


def _combine_kernel(ts_ref, code_ref, outs_ref, gw_ref, x1_ref, gf_ref, lnw_ref, lnb_ref, *rest, split):
    out_refs, (rows, sem) = rest[:-2], rest[-2:]

    def start(t, carry):
        for kk in range(TOP_K):
            src = _sorted_row(ts_ref, code_ref[0, t * TOP_K + kk])
            pltpu.make_async_copy(outs_ref.at[pl.ds(src, 1), :], rows.at[kk, pl.ds(t, 1), :], sem).start(priority=kk % 2)
        return carry

    def wait(t, carry):
        for kk in range(TOP_K):
            pltpu.make_async_copy(outs_ref.at[pl.ds(0, 1), :], rows.at[kk, pl.ds(0, 1), :], sem).wait()
        return carry

    lax.fori_loop(0, TM_COMB, start, 0)
    lax.fori_loop(0, TM_COMB, wait, 0)

    def emit(dst_ref):
        def sub(s, carry):
            r = pl.ds(pl.multiple_of(s * SUB_COMB, SUB_COMB), SUB_COMB)
            gw = gw_ref[r, :]
            y = gw[:, 0:1] * rows[0, r, :]
            for kk in range(1, TOP_K):
                y = y + gw[:, kk:kk + 1] * rows[kk, r, :]
            dst_ref[r, :] = _layer_norm(ALPHA * x1_ref[r, :] + gf_ref[...] * y, lnw_ref[...], lnb_ref[...])
            return carry

        lax.fori_loop(0, TM_COMB // SUB_COMB, sub, 0, unroll=2)

    if split:
        is_ctx = pl.program_id(0) < CTX_TILES_COMB
        pl.when(is_ctx)(lambda: emit(out_refs[0]))
        pl.when(jnp.logical_not(is_ctx))(lambda: emit(out_refs[1]))
    else:
        emit(out_refs[0])


def _combine(layer, outs, codes, tile_start, gw, x1, mods, ln_w, ln_b, *, split):
    nblk = T_ALL // TM_COMB
    if split:
        out_shape = (jax.ShapeDtypeStruct((T_CTX, D_MODEL), F32), jax.ShapeDtypeStruct((T_LAT, D_MODEL), F32))
        out_specs = (
            pl.BlockSpec((TM_COMB, D_MODEL), lambda i, *_: (jnp.minimum(i, CTX_TILES_COMB - 1), 0)),
            pl.BlockSpec((TM_COMB, D_MODEL), lambda i, *_: (jnp.maximum(i - CTX_TILES_COMB, 0), 0)),
        )
    else:
        out_shape = jax.ShapeDtypeStruct((T_ALL, D_MODEL), F32)
        out_specs = pl.BlockSpec((TM_COMB, D_MODEL), lambda i, *_: (i, 0))
    grid_spec = pltpu.PrefetchScalarGridSpec(
        num_scalar_prefetch=1,
        grid=(nblk,),
        in_specs=[
            pl.BlockSpec((None, 1, TM_COMB * TOP_K), lambda i, *_: (i, 0, 0), memory_space=pltpu.SMEM),
            pl.BlockSpec(memory_space=pl.ANY),
            pl.BlockSpec((TM_COMB, LANES), lambda i, *_: (i, 0)),
            pl.BlockSpec((TM_COMB, D_MODEL), lambda i, *_: (i, 0)),
            _mod_spec(layer, 5, TM_COMB),
            pl.BlockSpec((None, None, 1, D_MODEL), lambda i, *_: (layer, 1, 0, 0)),
            pl.BlockSpec((None, None, 1, D_MODEL), lambda i, *_: (layer, 1, 0, 0)),
        ],
        out_specs=out_specs,
        scratch_shapes=[pltpu.VMEM((TOP_K, TM_COMB, D_MODEL), F32), pltpu.SemaphoreType.DMA(())],
    )
    return pl.pallas_call(
        functools.partial(_combine_kernel, split=split),
        out_shape=out_shape,
        grid_spec=grid_spec,
        compiler_params=_cparams(1),
        name="moe_combine",
    )(tile_start, codes.reshape(nblk, 1, TM_COMB * TOP_K), outs, gw, x1, mods,
      ln_w.reshape(DEPTH, 2, 1, D_MODEL), ln_b.reshape(DEPTH, 2, 1, D_MODEL))


def _routing_tables(counts_f):
    counts = counts_f[0, :N_EXPERTS].astype(jnp.int32)
    ntiles = (counts + TM_MOE - 1) // TM_MOE
    tile_end = jnp.cumsum(ntiles)
    tile_start = tile_end - ntiles
    nchunks = (ntiles + TPC - 1) // TPC
    chunk_end = jnp.cumsum(nchunks)
    total = chunk_end[-1]
    cidx = jnp.arange(NC_MAX, dtype=jnp.int32)
    last = jnp.maximum(total - 1, 0)
    ce = jnp.sum((chunk_end[None, :] <= jnp.minimum(cidx, last)[:, None]).astype(jnp.int32), axis=1)
    ce = jnp.clip(ce, 0, N_EXPERTS - 1)
    kk = cidx - (chunk_end - nchunks)[ce]
    cs = tile_start[ce] + kk * TPC
    cn = jnp.where(cidx < total, jnp.clip(ntiles[ce] - kk * TPC, 0, TPC), 0)
    cs = jnp.where(cidx < total, cs, 0)
    meta = jnp.stack([total, tile_end[-1]]).astype(jnp.int32)
    tables = (ce.astype(jnp.int32), cs.astype(jnp.int32), cn.astype(jnp.int32), meta)
    return tile_start.astype(jnp.int32), tile_end.astype(jnp.int32), tables


def kernel(x_prompt, x_sample, c, cache_na_k, cache_na_v, state_mlstm_c, state_mlstm_n, state_mlstm_m,
           c_ctx, w_mod, b_mod, w_in, b_gate, mlstm_norm_w, na_rpb, w_out, ln_w, ln_b,
           router_w, router_b, expert_w1, expert_b1, expert_w2, expert_b2):
    g0 = 4 * W_M
    w_main = jnp.concatenate([w_in[:, :, :g0], w_in[:, :, g0 + N_GATES:]], axis=-1).astype(BF16)
    wg = w_in[:, :, g0:g0 + N_GATES].reshape(DEPTH, D_MODEL, 4, H_M).transpose(0, 1, 3, 2)
    w_gate = jnp.pad(wg, ((0, 0), (0, 0), (0, 0), (0, LANES - 4))).reshape(DEPTH, D_MODEL, GATE_W).astype(BF16)
    bg = b_gate.reshape(DEPTH, 4, H_M).transpose(0, 2, 1)
    bg = jnp.pad(bg, ((0, 0), (0, 0), (0, LANES - 4))).reshape(DEPTH, 1, GATE_W)
    w_out_b = w_out.astype(BF16)
    rw = jnp.pad(router_w, ((0, 0), (0, 0), (0, LANES - N_EXPERTS)))
    rb = jnp.pad(router_b, ((0, 0), (0, LANES - N_EXPERTS)), constant_values=NEG).reshape(DEPTH, 1, LANES)

    cvec = jnp.concatenate([c_ctx[None, :], c, jnp.zeros((MOD_ROWS - 1 - DEC_BATCH, D_MODEL), F32)], axis=0)
    mods = _adaln(cvec, w_mod, b_mod).reshape(DEPTH * MOD_ROWS * 6, 1, D_MODEL)

    x = jnp.concatenate([x_prompt.reshape(T_CTX, D_MODEL), x_sample.reshape(T_LAT, D_MODEL)], axis=0)
    ks_l, vs_l, cs_l, ns_l, ms_l = [], [], [], [], []
    for l in range(DEPTH):
        proj, gates, k_ctx, v_ctx = _in_proj(l, x, mods, w_main, w_gate, bg)
        m_ctx, st_c, st_n, st_m = _mlstm(l, proj, gates, mlstm_norm_w, latent=False)
        (m_lat,) = _mlstm(l, proj, gates, mlstm_norm_w, latent=True,
                          state=(state_mlstm_c, state_mlstm_n, state_mlstm_m))
        a_ctx = _attention(l, proj, latent=False)
        a_lat = _attention(l, proj, latent=True, cache_k=cache_na_k, cache_v=cache_na_v, rpb=na_rpb)
        x1, h2p, top_i, top_g = _out_proj(l, m_ctx, m_lat, a_ctx, a_lat, w_out_b, x, mods, ln_w, ln_b, rw, rb)
        code, counts = _route(top_i)
        tile_start, tile_end, tables = _routing_tables(counts)
        codes = code[:, :TOP_K].reshape(-1)
        xs = _scatter_rows(h2p, codes, tile_start, tile_end, tables[3])
        outs = _moe_experts(l, xs, tables, expert_w1, expert_b1, expert_w2, expert_b2)
        x = _combine(l, outs, codes, tile_start, top_g, x1, mods, ln_w, ln_b, split=(l == DEPTH - 1))

        ks_l.append(k_ctx.reshape(BATCH, SEQ, H_NA, D_NA))
        vs_l.append(v_ctx.reshape(BATCH, SEQ, H_NA, D_NA))
        cs_l.append(st_c)
        ns_l.append(st_n.reshape(BATCH, 2, H_M, DK_M))
        ms_l.append(st_m[:, :, :, 0, 0])

    y_prompt = x[0].reshape(BATCH, SEQ, D_MODEL)
    y_sample = x[1].reshape(DEC_BATCH, DEC_SEQ, D_MODEL)
    return (y_prompt, y_sample, jnp.stack(ks_l, axis=1), jnp.stack(vs_l, axis=1),
            jnp.stack(cs_l, axis=1), jnp.stack(ns_l, axis=1), jnp.stack(ms_l, axis=1))
```

```python
import functools

import numpy as np
import jax
import jax.numpy as jnp
from jax import lax
from jax.experimental import pallas as pl
from jax.experimental.pallas import tpu as pltpu

F32 = jnp.float32
BF16 = jnp.bfloat16

D_MODEL = 2048
BATCH, SEQ = 16, 256
DEC_BATCH, DEC_SEQ = 8, 1024
DEPTH = 2
PAST_LEN = 512
GRID_W = 64
H_M, DK_M, DV_M = 4, 256, 256
H_NA, D_NA = 8, 128
MAX_KH, KW = 8, 16
N_EXPERTS, TOP_K, D_FF = 32, 4, 2048
SWIGLU_LIMIT, SWIGLU_ALPHA = 7.0, 1.702
ROPE_BASE = 10000.0
CHUNK = 128
N_GATES = 4 * H_M
W_M = H_M * DV_M
W_NA = H_NA * D_NA
ALPHA = (2 * DEPTH) ** 0.25
NEG = -1e30
LN_EPS = 1e-5
HN_EPS = 1e-6

T_CTX = BATCH * SEQ
T_LAT = DEC_BATCH * DEC_SEQ
T_ALL = T_CTX + T_LAT
N_MAIN = 4 * W_M + 3 * W_NA
N_ASSIGN = T_ALL * TOP_K

LANES = 128
VMEM_LIMIT = 56 * 1024 * 1024

TM_IN = 1024
TN_IN = 512
TM_OUT = 256
TM_ROUTE = 256
TM_MOE = 256
CH_MOE = 2048
TPC = CH_MOE // TM_MOE
TF = 256
NJ = D_FF // TF
ROWS_SORTED = N_ASSIGN + N_EXPERTS * TM_MOE
NT_SORTED = ROWS_SORTED // TM_MOE
NC_MAX = N_EXPERTS + -(-NT_SORTED // TPC)
TM_COMB = 128
D_PACK = D_MODEL // 2
RANK_BITS = 16
assert N_ASSIGN < (1 << RANK_BITS)


def _cparams(n_axes, vmem=VMEM_LIMIT):
    return pltpu.CompilerParams(dimension_semantics=("arbitrary",) * n_axes, vmem_limit_bytes=vmem)


def _dot(a, b):
    return jnp.dot(a, b, preferred_element_type=F32)


def _split2(x):
    hi = x.astype(BF16)
    lo = (x - hi.astype(F32)).astype(BF16)
    return hi, lo


def _dot_f32(a, b):
    a1, a2 = _split2(a)
    b1, b2 = _split2(b)
    return _dot(a1, b1) + (_dot(a1, b2) + _dot(a2, b1))


def _layer_norm(z, w, b):
    mu = jnp.mean(z, axis=-1, keepdims=True)
    zc = z - mu
    var = jnp.mean(zc * zc, axis=-1, keepdims=True)
    return zc * lax.rsqrt(var + LN_EPS) * w + b


MOD_ROWS = 16
TN_MOD = 512


def _adaln_kernel(c_ref, w_ref, b_ref, o_ref):
    c = c_ref[...]
    s = c * jax.nn.sigmoid(c)
    o_ref[...] = _dot_f32(s, w_ref[...]) + b_ref[...]


def _adaln(cvec, w_mod, b_mod):
    n = 6 * D_MODEL
    return pl.pallas_call(
        _adaln_kernel,
        out_shape=jax.ShapeDtypeStruct((DEPTH, MOD_ROWS, n), F32),
        grid=(DEPTH, n // TN_MOD),
        in_specs=[
            pl.BlockSpec((MOD_ROWS, D_MODEL), lambda l, j: (0, 0)),
            pl.BlockSpec((None, D_MODEL, TN_MOD), lambda l, j: (l, 0, j)),
            pl.BlockSpec((None, 1, TN_MOD), lambda l, j: (l, 0, j)),
        ],
        out_specs=pl.BlockSpec((None, MOD_ROWS, TN_MOD), lambda l, j: (l, 0, j)),
        compiler_params=_cparams(2),
        name="adaln",
    )(cvec, w_mod, b_mod.reshape(DEPTH, 1, n))


def _mod_spec(layer, k, rows_per_tile):
    ctx_tiles = T_CTX // rows_per_tile
    tiles_per_lat = DEC_SEQ // rows_per_tile

    def index(i, *_):
        r = jnp.where(i < ctx_tiles, 0, 1 + (i - ctx_tiles) // tiles_per_lat)
        return ((layer * MOD_ROWS + r) * 6 + k, 0, 0)

    return pl.BlockSpec((None, 1, D_MODEL), index)


GATE_W = H_M * LANES


CTX_TILES_IN = T_CTX // TM_IN
SEQ_PER_TILE = TM_IN // SEQ
J_NK = (4 * W_M + W_NA) // TN_IN
J_NV = (4 * W_M + 2 * W_NA) // TN_IN
KV_TILES = W_NA // TN_IN


def _inproj_kernel(x_ref, sh_ref, sc_ref, w_ref, wg_ref, bg_ref, o_ref, g_ref, ko_ref, vo_ref, h_scr):
    i = pl.program_id(0)
    j = pl.program_id(1)

    @pl.when(j == 0)
    def _():
        h = (x_ref[...] * (1.0 + sc_ref[...]) + sh_ref[...]).astype(BF16)
        h_scr[...] = h
        g_ref[...] = _dot(h, wg_ref[...]) + bg_ref[...]

    out = _dot(h_scr[...], w_ref[...])
    o_ref[...] = out
    is_ctx = i < CTX_TILES_IN

    @pl.when(is_ctx & (j >= J_NK) & (j < J_NK + KV_TILES))
    def _():
        ko_ref[...] = out.reshape(SEQ_PER_TILE, SEQ, TN_IN)

    @pl.when(is_ctx & (j >= J_NV))
    def _():
        vo_ref[...] = out.reshape(SEQ_PER_TILE, SEQ, TN_IN)


def _kv_spec(j_first):
    def index(i, j):
        jt = jnp.clip(j - j_first, 0, KV_TILES - 1)
        ctx = i < CTX_TILES_IN
        return (jnp.where(ctx, i, CTX_TILES_IN - 1), 0, jnp.where(ctx, jt, KV_TILES - 1))

    return pl.BlockSpec((SEQ_PER_TILE, SEQ, TN_IN), index)


def _in_proj(layer, x, mods, w_main, w_gate, b_gate):
    kv_shape = jax.ShapeDtypeStruct((BATCH, SEQ, W_NA), F32)
    return pl.pallas_call(
        _inproj_kernel,
        out_shape=(jax.ShapeDtypeStruct((T_ALL, N_MAIN), F32),
                   jax.ShapeDtypeStruct((T_ALL, GATE_W), F32), kv_shape, kv_shape),
        grid=(T_ALL // TM_IN, N_MAIN // TN_IN),
        in_specs=[
            pl.BlockSpec((TM_IN, D_MODEL), lambda i, j: (i, 0)),
            _mod_spec(layer, 0, TM_IN),
            _mod_spec(layer, 1, TM_IN),
            pl.BlockSpec((None, D_MODEL, TN_IN), lambda i, j: (layer, 0, j)),
            pl.BlockSpec((None, D_MODEL, GATE_W), lambda i, j: (layer, 0, 0)),
            pl.BlockSpec((None, 1, GATE_W), lambda i, j: (layer, 0, 0)),
        ],
        out_specs=(pl.BlockSpec((TM_IN, TN_IN), lambda i, j: (i, j)),
                   pl.BlockSpec((TM_IN, GATE_W), lambda i, j: (i, 0)),
                   _kv_spec(J_NK), _kv_spec(J_NV)),
        scratch_shapes=[pltpu.VMEM((TM_IN, D_MODEL), BF16)],
        compiler_params=_cparams(2),
        name="in_proj",
    )(x, mods, mods, w_main, w_gate, b_gate)


def _log_sigmoid(x):
    return jnp.minimum(x, 0.0) - jnp.log(1.0 + jnp.exp(-jnp.abs(x)))


def _mlstm_kernel(*refs, seq, use_rope, has_init, emit_state, layer):
    it = iter(refs)
    q_ref, k_ref, v_ref, mo_ref, g_ref, nw_ref = (next(it) for _ in range(6))
    if use_rope:
        cos_ref, sin_ref = next(it), next(it)
    if has_init:
        c0_ref, n0_ref, m0_ref = next(it), next(it), next(it)
    out_ref = next(it)
    if emit_state:
        co_ref, no_ref, mso_ref = next(it), next(it), next(it)
    qs, ks, hs, c_scr = next(it), next(it), next(it), next(it)

    nc = seq // CHUNK
    b_idx = pl.program_id(0)
    h_idx = pl.program_id(1)

    def prep(j, carry):
        r0 = pl.multiple_of(j * CHUNK, CHUNK)
        rows = pl.ds(r0, CHUNK)
        q = q_ref[rows, :]
        k = k_ref[rows, :]
        if use_rope:
            cos = cos_ref[rows, :]
            sin = sin_ref[rows, :]

            def rope(x):
                halves = []
                for p in range(2):
                    xp = x[:, p * LANES:(p + 1) * LANES]
                    halves.append(xp * cos[:, p * LANES:(p + 1) * LANES]
                                  + pltpu.roll(xp, LANES // 2, axis=1) * sin[:, p * LANES:(p + 1) * LANES])
                return jnp.concatenate(halves, axis=1)

            q = rope(q)
            k = rope(k)
        qs[rows, :] = (q * (DK_M ** -0.5)).astype(BF16)
        ks[rows, :] = k.astype(BF16)
        return carry

    lax.fori_loop(0, nc, prep, 0)

    row_i = lax.broadcasted_iota(jnp.int32, (CHUNK, CHUNK), 0)
    col_i = lax.broadcasted_iota(jnp.int32, (CHUNK, CHUNK), 1)

    def chunk_rows(j):
        return pl.ds(pl.multiple_of(j * CHUNK, CHUNK), CHUNK)

    init = []
    for d in range(2):
        if has_init:
            c_scr[d] = c0_ref[d]
            init += [n0_ref[d], jnp.full((1, 1), m0_ref[((b_idx * DEPTH + layer) * 2 + d) * H_M + h_idx], F32)]
        else:
            c_scr[d] = jnp.zeros((DK_M, DV_M), F32)
            init += [jnp.zeros((1, DK_M), F32), jnp.full((1, 1), NEG, F32)]

    def chunk_step(d, j, n, m):
        mask = (col_i <= row_i) if d == 0 else (col_i >= row_i)
        tri = mask.astype(BF16)
        lane_i, lane_b = 2 * d, 2 * d + 1
        end_row = CHUNK - 1 if d == 0 else 0
        rows = chunk_rows(j)
        gates = g_ref[rows, :]
        lf = _log_sigmoid(gates)
        lf1, lf2 = _split2(lf)
        lf3 = (lf - lf1.astype(F32) - lf2.astype(F32)).astype(BF16)
        bc = _dot(tri, lf1) + (_dot(tri, lf2) + _dot(tri, lf3))
        gates_t = gates.T
        bc_t = bc.T
        b_col = bc[:, lane_b:lane_b + 1]
        ig_col = gates[:, lane_i:lane_i + 1]
        b_row = bc_t[lane_b:lane_b + 1, :]
        ig_row = gates_t[lane_i:lane_i + 1, :]
        b_end = bc[end_row:end_row + 1, lane_b:lane_b + 1]

        dlog = jnp.where(mask, (b_col - b_row) + ig_row, NEG)
        g = b_col + m
        mt = jnp.maximum(g, jnp.max(dlog, axis=1, keepdims=True))
        qc = qs[rows, :]
        kc = ks[rows, :]
        vc = v_ref[rows, :].astype(BF16)
        s = lax.dot_general(qc, kc, (((1,), (1,)), ((), ())), preferred_element_type=F32)
        s = s * jnp.exp(dlog - mt)
        sg = jnp.exp(g - mt)
        c_old = c_scr[d]
        num = _dot(s.astype(BF16), vc) + sg * _dot(qc, c_old.astype(BF16))
        den = (jnp.sum(s, axis=1, keepdims=True)
               + sg * jnp.sum(qc.astype(F32) * n, axis=1, keepdims=True))
        h = num / jnp.maximum(jnp.abs(den), jnp.exp(-mt))

        wlog = (b_end - b_col) + ig_col
        m_new = jnp.maximum(b_end + m, jnp.max(wlog, axis=0, keepdims=True))
        ws = jnp.exp(wlog - m_new)
        sc = jnp.exp(b_end + m - m_new)
        kw = kc.astype(F32) * ws
        c_scr[d] = sc * c_old + _dot(kw.T.astype(BF16), vc)
        n_new = sc * n + jnp.sum(kw, axis=0, keepdims=True)
        return h, n_new, m_new

    def finish(j, ht):
        rows = chunk_rows(j)
        mu = jnp.mean(ht, axis=1, keepdims=True)
        hc = ht - mu
        var = jnp.mean(hc * hc, axis=1, keepdims=True)
        hn = hc * lax.rsqrt(var + HN_EPS) * nw_ref[...]
        out_ref[rows, :] = (jax.nn.sigmoid(mo_ref[rows, :]) * hn).astype(out_ref.dtype)

    def first_half(jj, carry):
        nf, mf, nb, mb = carry
        jb = nc - 1 - jj
        hf, nf, mf = chunk_step(0, jj, nf, mf)
        hb, nb, mb = chunk_step(1, jb, nb, mb)
        hs[chunk_rows(jj), :] = hf
        hs[chunk_rows(jb), :] = hb
        return nf, mf, nb, mb

    def second_half(jj, carry):
        nf, mf, nb, mb = carry
        jb = nc - 1 - jj
        hf, nf, mf = chunk_step(0, jj, nf, mf)
        hb, nb, mb = chunk_step(1, jb, nb, mb)
        finish(jj, hs[chunk_rows(jj), :] + hf)
        finish(jb, hs[chunk_rows(jb), :] + hb)
        return nf, mf, nb, mb

    carry = lax.fori_loop(0, nc // 2, first_half, tuple(init))
    n_f, m_f, n_b, m_b = lax.fori_loop(nc // 2, nc, second_half, carry)
    if emit_state:
        for d, (n_fin, m_fin) in enumerate(((n_f, m_f), (n_b, m_b))):
            co_ref[d] = c_scr[d]
            no_ref[d] = n_fin
            mso_ref[d] = jnp.broadcast_to(m_fin, (1, LANES))


def _rope_tables(seq):
    pos = np.arange(seq)
    half = DK_M // 2
    nf = half // 2
    inv = ROPE_BASE ** (-np.arange(nf, dtype=np.float32) / nf)
    cos_parts, sin_parts = [], []
    for p in ((pos // GRID_W).astype(np.float32), (pos % GRID_W).astype(np.float32)):
        ang = (p[:, None] * inv).astype(np.float32)
        cos_parts += [np.cos(ang), np.cos(ang)]
        sin_parts += [-np.sin(ang), np.sin(ang)]
    return (jnp.asarray(np.concatenate(cos_parts, axis=1), F32),
            jnp.asarray(np.concatenate(sin_parts, axis=1), F32))


def _mlstm(layer, proj, gates, norm_w, *, latent, state=None):
    if latent:
        nb, seq, rb0 = DEC_BATCH, DEC_SEQ, T_CTX // DEC_SEQ
    else:
        nb, seq, rb0 = BATCH, SEQ, 0
    cb = W_M // DK_M
    in_specs = [
        pl.BlockSpec((seq, DK_M), lambda b, h: (rb0 + b, h)),
        pl.BlockSpec((seq, DK_M), lambda b, h: (rb0 + b, cb + h)),
        pl.BlockSpec((seq, DV_M), lambda b, h: (rb0 + b, 2 * cb + h)),
        pl.BlockSpec((seq, DV_M), lambda b, h: (rb0 + b, 3 * cb + h)),
        pl.BlockSpec((seq, LANES), lambda b, h: (rb0 + b, h)),
        pl.BlockSpec((None, 1, DV_M), lambda b, h: (layer, 0, h)),
    ]
    args = [proj, proj, proj, proj, gates, norm_w.reshape(DEPTH, 1, W_M)]
    if latent:
        cos, sin = _rope_tables(seq)
        in_specs += [pl.BlockSpec((seq, DK_M), lambda b, h: (0, 0))] * 2
        args += [cos, sin]
        st_c, st_n, st_m = state
        in_specs += [
            pl.BlockSpec((None, None, 2, None, DK_M, DV_M), lambda b, h: (b, layer, 0, h, 0, 0)),
            pl.BlockSpec((None, None, 2, None, 1, DK_M), lambda b, h: (b, layer, 0, h, 0, 0)),
            pl.BlockSpec(memory_space=pltpu.SMEM),
        ]
        args += [st_c, st_n.reshape(DEC_BATCH, DEPTH, 2, H_M, 1, DK_M), st_m.reshape(-1)]
    out_shape = [jax.ShapeDtypeStruct((nb * seq, W_M), BF16)]
    out_specs = [pl.BlockSpec((seq, DV_M), lambda b, h: (b, h))]
    if not latent:
        out_shape += [
            jax.ShapeDtypeStruct((nb, 2, H_M, DK_M, DV_M), F32),
            jax.ShapeDtypeStruct((nb, 2, H_M, 1, DK_M), F32),
            jax.ShapeDtypeStruct((nb, 2, H_M, 1, LANES), F32),
        ]
        out_specs += [
            pl.BlockSpec((None, 2, None, DK_M, DV_M), lambda b, h: (b, 0, h, 0, 0)),
            pl.BlockSpec((None, 2, None, 1, DK_M), lambda b, h: (b, 0, h, 0, 0)),
            pl.BlockSpec((None, 2, None, 1, LANES), lambda b, h: (b, 0, h, 0, 0)),
        ]
    kern = functools.partial(_mlstm_kernel, seq=seq, use_rope=latent, has_init=latent,
                             emit_state=not latent, layer=layer)
    return pl.pallas_call(
        kern,
        out_shape=tuple(out_shape),
        grid=(nb, H_M),
        in_specs=in_specs,
        out_specs=tuple(out_specs),
        scratch_shapes=[
            pltpu.VMEM((seq, DK_M), BF16),
            pltpu.VMEM((seq, DK_M), BF16),
            pltpu.VMEM((seq, DV_M), F32),
            pltpu.VMEM((2, DK_M, DV_M), F32),
        ],
        compiler_params=_cparams(2),
        name="mlstm_lat" if latent else "mlstm_ctx",
    )(*args)


RPB_ROWS = 2 * MAX_KH - 1
RPB_COLS = 2 * KW - 1
Q_TILE = 256
LAT_ROWS = DEC_SEQ // GRID_W
KH = min(MAX_KH, LAT_ROWS)


def _attn_ctx_kernel(q_ref, k_ref, v_ref, o_ref):
    scale = D_NA ** -0.5
    q = q_ref[...].astype(BF16)
    k = k_ref[...].astype(BF16)
    v = v_ref[...].astype(BF16)
    s = lax.dot_general(q, k, (((1,), (1,)), ((), ())), preferred_element_type=F32) * scale
    m = jnp.max(s, axis=1, keepdims=True)
    p = jnp.exp(s - m)
    l = jnp.sum(p, axis=1, keepdims=True)
    o_ref[...] = (_dot(p.astype(BF16), v) / l).astype(o_ref.dtype)


def _attn_lat_kernel(rpb_ref, q_ref, k_ref, v_ref, kc_ref, vc_ref, o_ref, bias_scr, g_scr):
    h_idx = pl.program_id(0)
    scale = D_NA ** -0.5

    @pl.when(pl.program_id(1) == 0)
    def _build_bias():
        qc = lax.broadcasted_iota(jnp.int32, (GRID_W, LANES), 0)
        lane = lax.broadcasted_iota(jnp.int32, (GRID_W, LANES), 1)
        kc = lane & (GRID_W - 1)
        upper = lane >= GRID_W
        cidx = jnp.clip(kc - qc, -(KW - 1), KW - 1) + (KW - 1)
        cs = jnp.clip(qc - KW // 2, 0, GRID_W - KW)
        col_ok = (kc >= cs) & (kc < cs + KW)
        base = h_idx * (RPB_ROWS * RPB_COLS)
        for dr0 in range(-KH, KH):
            acc = jnp.zeros((GRID_W, LANES), F32)
            for c in range(RPB_COLS):
                lo = rpb_ref[base + (dr0 + MAX_KH - 1) * RPB_COLS + c] if -MAX_KH < dr0 < MAX_KH else 0.0
                hi = rpb_ref[base + (dr0 + MAX_KH) * RPB_COLS + c] if -MAX_KH < dr0 + 1 < MAX_KH else 0.0
                acc = jnp.where(cidx == c, jnp.where(upper, hi, lo), acc)
            g_scr[dr0 + KH] = jnp.where(col_ok, acc, NEG)
        neg_tile = jnp.full((GRID_W, LANES), NEG, F32)
        for qr in range(LAT_ROWS):
            rs = min(max(qr - KH // 2, 0), LAT_ROWS - KH)
            for t in range(LAT_ROWS // 2):
                ok0 = rs <= 2 * t < rs + KH
                ok1 = rs <= 2 * t + 1 < rs + KH
                if ok0 or ok1:
                    tile = g_scr[2 * t - qr + KH]
                    if not ok1:
                        tile = jnp.where(upper, NEG, tile)
                    if not ok0:
                        tile = jnp.where(upper, tile, NEG)
                else:
                    tile = neg_tile
                bias_scr[qr * GRID_W:(qr + 1) * GRID_W, t * LANES:(t + 1) * LANES] = tile

    k = k_ref[...].astype(BF16)
    v = v_ref[...].astype(BF16)
    kc = kc_ref[...].astype(BF16)
    vc = vc_ref[...].astype(BF16)
    dn = (((1,), (1,)), ((), ()))
    for qt in range(DEC_SEQ // Q_TILE):
        rows = slice(qt * Q_TILE, (qt + 1) * Q_TILE)
        q = q_ref[rows, :].astype(BF16)
        s_loc = lax.dot_general(q, k, dn, preferred_element_type=F32) * scale + bias_scr[rows, :]
        s_ctx = lax.dot_general(q, kc, dn, preferred_element_type=F32) * scale
        m = jnp.maximum(jnp.max(s_loc, axis=1, keepdims=True), jnp.max(s_ctx, axis=1, keepdims=True))
        p_loc = jnp.exp(s_loc - m)
        p_ctx = jnp.exp(s_ctx - m)
        l = jnp.sum(p_loc, axis=1, keepdims=True) + jnp.sum(p_ctx, axis=1, keepdims=True)
        o = _dot(p_loc.astype(BF16), v) + _dot(p_ctx.astype(BF16), vc)
        o_ref[rows, :] = (o / l).astype(o_ref.dtype)


def _attention(layer, proj, *, latent, cache_k=None, cache_v=None, rpb=None):
    qb0 = 4 * W_M // D_NA
    if not latent:
        return pl.pallas_call(
            _attn_ctx_kernel,
            out_shape=jax.ShapeDtypeStruct((T_CTX, W_NA), BF16),
            grid=(BATCH, H_NA),
            in_specs=[
                pl.BlockSpec((SEQ, D_NA), lambda b, h: (b, qb0 + h)),
                pl.BlockSpec((SEQ, D_NA), lambda b, h: (b, qb0 + H_NA + h)),
                pl.BlockSpec((SEQ, D_NA), lambda b, h: (b, qb0 + 2 * H_NA + h)),
            ],
            out_specs=pl.BlockSpec((SEQ, D_NA), lambda b, h: (b, h)),
            compiler_params=_cparams(2),
            name="attn_ctx",
        )(proj, proj, proj)
    rb0 = T_CTX // DEC_SEQ
    ck = cache_k.reshape(DEC_BATCH, DEPTH, PAST_LEN, W_NA)
    cv = cache_v.reshape(DEC_BATCH, DEPTH, PAST_LEN, W_NA)
    grid_spec = pltpu.PrefetchScalarGridSpec(
        num_scalar_prefetch=1,
        grid=(H_NA, DEC_BATCH),
        in_specs=[
            pl.BlockSpec((DEC_SEQ, D_NA), lambda h, b, r: (rb0 + b, qb0 + h)),
            pl.BlockSpec((DEC_SEQ, D_NA), lambda h, b, r: (rb0 + b, qb0 + H_NA + h)),
            pl.BlockSpec((DEC_SEQ, D_NA), lambda h, b, r: (rb0 + b, qb0 + 2 * H_NA + h)),
            pl.BlockSpec((None, None, PAST_LEN, D_NA), lambda h, b, r: (b, layer, 0, h)),
            pl.BlockSpec((None, None, PAST_LEN, D_NA), lambda h, b, r: (b, layer, 0, h)),
        ],
        out_specs=pl.BlockSpec((DEC_SEQ, D_NA), lambda h, b, r: (b, h)),
        scratch_shapes=[
            pltpu.VMEM((DEC_SEQ, DEC_SEQ), F32),
            pltpu.VMEM((2 * KH, GRID_W, LANES), F32),
        ],
    )
    return pl.pallas_call(
        _attn_lat_kernel,
        out_shape=jax.ShapeDtypeStruct((T_LAT, W_NA), BF16),
        grid_spec=grid_spec,
        compiler_params=_cparams(2),
        name="attn_lat",
    )(rpb[layer].reshape(-1), proj, proj, proj, ck, cv)


def _outproj_kernel(mc_ref, ml_ref, ac_ref, al_ref, w_ref, x_ref, ga_ref, shf_ref, scf_ref,
                    lnw_ref, lnb_ref, rw_ref, rb_ref, x1_ref, h2_ref, ti_ref, tg_ref):
    is_ctx = pl.program_id(0) < T_CTX // TM_OUT
    m = jnp.where(is_ctx, mc_ref[...], ml_ref[...])
    a = jnp.where(is_ctx, ac_ref[...], al_ref[...])
    y = _dot(m, w_ref[0:W_M, :]) + _dot(a, w_ref[W_M:W_M + W_NA, :])
    x1 = _layer_norm(ALPHA * x_ref[...] + ga_ref[...] * y, lnw_ref[...], lnb_ref[...])
    x1_ref[...] = x1
    h2 = x1 * (1.0 + scf_ref[...]) + shf_ref[...]
    bits = lax.bitcast_convert_type(h2.astype(BF16).astype(F32), jnp.uint32)
    h2_ref[...] = (bits[:, :D_PACK] >> 16) | (bits[:, D_PACK:] & jnp.uint32(0xFFFF0000))
    logits = _dot_f32(h2, rw_ref[...]) + rb_ref[...]
    lane = lax.broadcasted_iota(jnp.int32, logits.shape, 1)
    vals = logits
    top_v, top_i = [], []
    for _ in range(TOP_K):
        mx = jnp.max(vals, axis=1, keepdims=True)
        ix = jnp.min(jnp.where(vals == mx, lane, LANES), axis=1, keepdims=True)
        top_v.append(mx)
        top_i.append(ix)
        vals = jnp.where(lane == ix, -jnp.inf, vals)
    ex = [jnp.exp(v - top_v[0]) for v in top_v]
    den = ex[0] + ex[1] + ex[2] + ex[3]
    ti = jnp.zeros(logits.shape, jnp.int32)
    tg = jnp.zeros(logits.shape, F32)
    for kk in range(TOP_K):
        ti = jnp.where(lane == kk, top_i[kk], ti)
        tg = jnp.where(lane == kk, ex[kk] / den, tg)
    ti_ref[...] = ti
    tg_ref[...] = tg


def _out_proj(layer, m_ctx, m_lat, a_ctx, a_lat, w_out, x, mods, ln_w, ln_b, router_w, router_b):
    nctx = T_CTX // TM_OUT
    ctx_spec = lambda w: pl.BlockSpec((TM_OUT, w), lambda i: (jnp.minimum(i, nctx - 1), 0))
    lat_spec = lambda w: pl.BlockSpec((TM_OUT, w), lambda i: (jnp.maximum(i - nctx, 0), 0))
    row_spec = lambda w: pl.BlockSpec((TM_OUT, w), lambda i: (i, 0))
    return pl.pallas_call(
        _outproj_kernel,
        out_shape=(jax.ShapeDtypeStruct((T_ALL, D_MODEL), F32),
                   jax.ShapeDtypeStruct((T_ALL, D_PACK), jnp.uint32),
                   jax.ShapeDtypeStruct((T_ALL, LANES), jnp.int32),
                   jax.ShapeDtypeStruct((T_ALL, LANES), F32)),
        grid=(T_ALL // TM_OUT,),
        in_specs=[
            ctx_spec(W_M), lat_spec(W_M), ctx_spec(W_NA), lat_spec(W_NA),
            pl.BlockSpec((None, W_M + W_NA, D_MODEL), lambda i: (layer, 0, 0)),
            row_spec(D_MODEL),
            _mod_spec(layer, 2, TM_OUT), _mod_spec(layer, 3, TM_OUT), _mod_spec(layer, 4, TM_OUT),
            pl.BlockSpec((None, None, 1, D_MODEL), lambda i: (layer, 0, 0, 0)),
            pl.BlockSpec((None, None, 1, D_MODEL), lambda i: (layer, 0, 0, 0)),
            pl.BlockSpec((None, D_MODEL, LANES), lambda i: (layer, 0, 0)),
            pl.BlockSpec((None, 1, LANES), lambda i: (layer, 0, 0)),
        ],
        out_specs=(row_spec(D_MODEL), row_spec(D_PACK), row_spec(LANES), row_spec(LANES)),
        compiler_params=_cparams(1),
        name="out_proj",
    )(m_ctx, m_lat, a_ctx, a_lat, w_out, x, mods, mods, mods,
      ln_w.reshape(DEPTH, 2, 1, D_MODEL), ln_b.reshape(DEPTH, 2, 1, D_MODEL), router_w, router_b)


def _route_kernel(ti_ref, code_ref, cnt_ref, carry):
    @pl.when(pl.program_id(0) == 0)
    def _():
        carry[...] = jnp.zeros((1, LANES), F32)

    ti = ti_ref[...]
    lane = lax.broadcasted_iota(jnp.int32, (TM_ROUTE, LANES), 1)
    hits = [lane == ti[:, kk:kk + 1] for kk in range(TOP_K)]
    cnt = hits[0].astype(F32)
    for kk in range(1, TOP_K):
        cnt = cnt + hits[kk].astype(F32)
    row_i = lax.broadcasted_iota(jnp.int32, (TM_ROUTE, TM_ROUTE), 0)
    col_i = lax.broadcasted_iota(jnp.int32, (TM_ROUTE, TM_ROUTE), 1)
    before = _dot((col_i < row_i).astype(BF16), cnt.astype(BF16)) + carry[...]
    code = jnp.zeros((TM_ROUTE, LANES), jnp.int32)
    for kk in range(TOP_K):
        rank = jnp.sum(jnp.where(hits[kk], before, 0.0), axis=1, keepdims=True).astype(jnp.int32)
        code = jnp.where(lane == kk, (ti[:, kk:kk + 1] << RANK_BITS) | rank, code)
    code_ref[...] = code
    carry[...] = carry[...] + jnp.sum(cnt, axis=0, keepdims=True)
    cnt_ref[...] = carry[...]


def _route(top_i):
    return pl.pallas_call(
        _route_kernel,
        out_shape=(jax.ShapeDtypeStruct((T_ALL, LANES), jnp.int32),
                   jax.ShapeDtypeStruct((1, LANES), F32)),
        grid=(T_ALL // TM_ROUTE,),
        in_specs=[pl.BlockSpec((TM_ROUTE, LANES), lambda i: (i, 0))],
        out_specs=(pl.BlockSpec((TM_ROUTE, LANES), lambda i: (i, 0)),
                   pl.BlockSpec((1, LANES), lambda i: (0, 0))),
        scratch_shapes=[pltpu.VMEM((1, LANES), F32)],
        compiler_params=_cparams(1),
        name="moe_route",
    )(top_i)


def _sorted_row(ts_ref, code):
    return ts_ref[code >> RANK_BITS] * TM_MOE + (code & ((1 << RANK_BITS) - 1))


def _scatter_kernel(ts_ref, te_ref, meta_ref, code_ref, src_ref, xs_ref, zero_scr, sem):
    @pl.when(pl.program_id(0) == 0)
    def _define_padding():
        zero_scr[...] = jnp.zeros((TM_MOE, D_PACK), jnp.uint32)

        def fill(t):
            dst = xs_ref.at[pl.ds(pl.multiple_of(t * TM_MOE, TM_MOE), TM_MOE), :]
            return pltpu.make_async_copy(zero_scr, dst, sem)

        def per_expert(fn):
            def body(e, carry):
                @pl.when(te_ref[e] > ts_ref[e])
                def _():
                    fn(fill(te_ref[e] - 1))
                return carry
            lax.fori_loop(0, N_EXPERTS, body, 0)

        def tail(fn):
            def body(t, carry):
                fn(fill(t))
                return carry
            lax.fori_loop(meta_ref[1], NT_SORTED, body, 0)

        per_expert(lambda cp: cp.start())
        tail(lambda cp: cp.start())
        per_expert(lambda cp: cp.wait())
        tail(lambda cp: cp.wait())

    def start(t, carry):
        for kk in range(TOP_K):
            dst = _sorted_row(ts_ref, code_ref[0, t * TOP_K + kk])
            pltpu.make_async_copy(src_ref.at[pl.ds(t, 1), :], xs_ref.at[pl.ds(dst, 1), :], sem).start(priority=kk % 2)
        return carry

    def wait(t, carry):
        for kk in range(TOP_K):
            pltpu.make_async_copy(src_ref.at[pl.ds(0, 1), :], xs_ref.at[pl.ds(0, 1), :], sem).wait()
        return carry

    lax.fori_loop(0, TM_ROUTE, start, 0, unroll=2)
    lax.fori_loop(0, TM_ROUTE, wait, 0)


def _scatter_rows(src, codes, tile_start, tile_end, meta):
    nblk = T_ALL // TM_ROUTE
    grid_spec = pltpu.PrefetchScalarGridSpec(
        num_scalar_prefetch=3,
        grid=(nblk,),
        in_specs=[
            pl.BlockSpec((None, 1, TM_ROUTE * TOP_K), lambda i, *_: (i, 0, 0), memory_space=pltpu.SMEM),
            pl.BlockSpec((TM_ROUTE, D_PACK), lambda i, *_: (i, 0)),
        ],
        out_specs=pl.BlockSpec(memory_space=pl.ANY),
        scratch_shapes=[pltpu.VMEM((TM_MOE, D_PACK), jnp.uint32), pltpu.SemaphoreType.DMA(())],
    )
    return pl.pallas_call(
        _scatter_kernel,
        out_shape=jax.ShapeDtypeStruct((ROWS_SORTED, D_PACK), jnp.uint32),
        grid_spec=grid_spec,
        compiler_params=_cparams(1),
        name="moe_scatter",
    )(tile_start, tile_end, meta, codes.reshape(nblk, 1, TM_ROUTE * TOP_K), src)


def _moe_kernel(ce_ref, cs_ref, cn_ref, nch_ref, xs_ref, w1_ref, b1_ref, w2_ref, b2_ref, sel_ref,
                out_ref, xbuf, acc, w1b, w2b, stage, sem_in, sem_out):
    c = pl.program_id(0)
    j = pl.program_id(1)
    nt = cn_ref[c]
    row0 = cs_ref[c] * TM_MOE

    def tile_rows(r):
        return pl.ds(pl.multiple_of(r * TM_MOE, TM_MOE), TM_MOE)

    @pl.when(nt > 0)
    def _chunk():
        @pl.when(j == 0)
        def _load_rows():
            def fetch(r, slot):
                src = xs_ref.at[pl.ds(pl.multiple_of(row0 + r * TM_MOE, TM_MOE), TM_MOE), :]
                return pltpu.make_async_copy(src, stage.at[slot], sem_in.at[slot])

            fetch(0, 0).start()

            def load(r, carry):
                slot = r & 1
                fetch(r, slot).wait()

                @pl.when(r + 1 < nt)
                def _():
                    fetch(r + 1, 1 - slot).start()

                words = stage[slot]
                lo = lax.bitcast_convert_type(words << 16, F32)
                hi = lax.bitcast_convert_type(words & jnp.uint32(0xFFFF0000), F32)
                xbuf[tile_rows(r), 0:D_PACK] = lo.astype(BF16)
                xbuf[tile_rows(r), D_PACK:D_MODEL] = hi.astype(BF16)
                acc[tile_rows(r), :] = jnp.broadcast_to(b2_ref[...], (TM_MOE, D_MODEL))
                return carry

            lax.fori_loop(0, nt, load, 0)

        w1b[...] = w1_ref[...].astype(BF16)
        w2b[...] = w2_ref[...].astype(BF16)

        def out_copy(start):
            rows = pl.ds(pl.multiple_of(start, TM_MOE), TM_MOE)
            dst = out_ref.at[pl.ds(pl.multiple_of(row0 + start, TM_MOE), TM_MOE), :]
            return pltpu.make_async_copy(acc.at[rows, :], dst, sem_out)

        def tile(start, size):
            rows = pl.ds(pl.multiple_of(start, TM_MOE), size)
            hdn = _dot(xbuf[rows, :], w1b[...]) + b1_ref[...]
            glu = jnp.minimum(hdn, SWIGLU_LIMIT)
            lin = jnp.clip(hdn, -SWIGLU_LIMIT, SWIGLU_LIMIT) + 1.0
            z = glu * jax.nn.sigmoid(SWIGLU_ALPHA * glu) * pltpu.roll(lin, 2 * TF - 1, axis=1)
            act = _dot(z.astype(BF16), sel_ref[...]).astype(BF16)
            acc[rows, :] = acc[rows, :] + _dot(act, w2b[...])

            @pl.when(j == NJ - 1)
            def _():
                for g in range(size // TM_MOE):
                    out_copy(start + g * TM_MOE).start()

        def pair(r, carry):
            tile(r * (2 * TM_MOE), 2 * TM_MOE)
            return carry

        lax.fori_loop(0, nt >> 1, pair, 0)

        @pl.when((nt & 1) == 1)
        def _():
            tile((nt - 1) * TM_MOE, TM_MOE)

        @pl.when(j == NJ - 1)
        def _drain_row_stores():
            def wait(r, carry):
                out_copy(r * TM_MOE).wait()
                return carry

            lax.fori_loop(0, nt, wait, 0)

    @pl.when((c == NC_MAX - 1) & (j == NJ - 1))
    def _zero_unused_tiles():
        zero_rows = acc.at[pl.ds(0, TM_MOE), :]
        zero_rows[...] = jnp.zeros((TM_MOE, D_MODEL), F32)

        def copy(t):
            dst = out_ref.at[pl.ds(pl.multiple_of(t * TM_MOE, TM_MOE), TM_MOE), :]
            return pltpu.make_async_copy(zero_rows, dst, sem_out)

        def start(t, carry):
            copy(t).start()
            return carry

        def wait(t, carry):
            copy(t).wait()
            return carry

        lax.fori_loop(nch_ref[1], NT_SORTED, start, 0)
        lax.fori_loop(nch_ref[1], NT_SORTED, wait, 0)


def _moe_experts(layer, xs, tables, w1, b1, w2, b2):
    ce, cs, cn, nch = tables
    sel = np.zeros((2 * TF, TF), np.float32)
    sel[2 * np.arange(TF), np.arange(TF)] = 1.0

    def jeff(c, j, nch_ref):
        return jnp.where(c < nch_ref[0], j, NJ - 1)

    grid_spec = pltpu.PrefetchScalarGridSpec(
        num_scalar_prefetch=4,
        grid=(NC_MAX, NJ),
        in_specs=[
            pl.BlockSpec(memory_space=pl.ANY),
            pl.BlockSpec((None, None, D_MODEL, 2 * TF),
                         lambda c, j, ce, cs, cn, nch: (layer, ce[c], 0, jeff(c, j, nch))),
            pl.BlockSpec((None, None, 1, 2 * TF),
                         lambda c, j, ce, cs, cn, nch: (layer, ce[c], 0, jeff(c, j, nch))),
            pl.BlockSpec((None, None, TF, D_MODEL),
                         lambda c, j, ce, cs, cn, nch: (layer, ce[c], jeff(c, j, nch), 0)),
            pl.BlockSpec((None, None, 1, D_MODEL),
                         lambda c, j, ce, cs, cn, nch: (layer, ce[c], 0, 0)),
            pl.BlockSpec((2 * TF, TF), lambda c, j, ce, cs, cn, nch: (0, 0)),
        ],
        out_specs=pl.BlockSpec(memory_space=pl.ANY),
        scratch_shapes=[
            pltpu.VMEM((CH_MOE, D_MODEL), BF16),
            pltpu.VMEM((CH_MOE, D_MODEL), F32),
            pltpu.VMEM((D_MODEL, 2 * TF), BF16),
            pltpu.VMEM((TF, D_MODEL), BF16),
            pltpu.VMEM((2, TM_MOE, D_PACK), jnp.uint32),
            pltpu.SemaphoreType.DMA((2,)),
            pltpu.SemaphoreType.DMA(()),
        ],
    )
    return pl.pallas_call(
        _moe_kernel,
        out_shape=jax.ShapeDtypeStruct((ROWS_SORTED, D_MODEL), F32),
        grid_spec=grid_spec,
        compiler_params=_cparams(2),
        name="moe_experts",
    )(ce, cs, cn, nch, xs, w1, b1.reshape(DEPTH, N_EXPERTS, 1, 2 * D_FF), w2,
      b2.reshape(DEPTH, N_EXPERTS, 1, D_MODEL), jnp.asarray(sel, BF16))


SUB_COMB = 16
PARTS_COMB = 4
CTX_TILES_COMB = T_CTX // TM_COMB


def _combine_kernel(ts_ref, code_ref, outs_ref, gw_ref, x1_ref, gf_ref, lnw_ref, lnb_ref, *rest, split):
    out_refs, (rows, sem) = rest[:-2], rest[-2:]
    part = TM_COMB // PARTS_COMB

    def start(t, carry):
        for kk in range(TOP_K):
            src = _sorted_row(ts_ref, code_ref[0, t * TOP_K + kk])
            pltpu.make_async_copy(outs_ref.at[pl.ds(src, 1), :], rows.at[kk, pl.ds(t, 1), :],
                                  sem.at[lax.shift_right_logical(t, part.bit_length() - 1)]).start(priority=kk % 2)
        return carry

    lax.fori_loop(0, TM_COMB, start, 0, unroll=2)

    def emit(dst_ref):
        for p in range(PARTS_COMB):
            def wait(t, carry, p=p):
                for kk in range(TOP_K):
                    pltpu.make_async_copy(outs_ref.at[pl.ds(0, 1), :], rows.at[kk, pl.ds(0, 1), :], sem.at[p]).wait()
                return carry

            lax.fori_loop(0, part, wait, 0)

            def sub(s, carry, p=p):
                r = pl.ds(pl.multiple_of(p * part + s * SUB_COMB, SUB_COMB), SUB_COMB)
                gw = gw_ref[r, :]
                y = gw[:, 0:1] * rows[0, r, :]
                for kk in range(1, TOP_K):
                    y = y + gw[:, kk:kk + 1] * rows[kk, r, :]
                dst_ref[r, :] = _layer_norm(ALPHA * x1_ref[r, :] + gf_ref[...] * y, lnw_ref[...], lnb_ref[...])
                return carry

            lax.fori_loop(0, part // SUB_COMB, sub, 0, unroll=2)

    if split:
        is_ctx = pl.program_id(0) < CTX_TILES_COMB
        pl.when(is_ctx)(lambda: emit(out_refs[0]))
        pl.when(jnp.logical_not(is_ctx))(lambda: emit(out_refs[1]))
    else:
        emit(out_refs[0])


def _combine(layer, outs, codes, tile_start, gw, x1, mods, ln_w, ln_b, *, split):
    nblk = T_ALL // TM_COMB
    if split:
        out_shape = (jax.ShapeDtypeStruct((T_CTX, D_MODEL), F32), jax.ShapeDtypeStruct((T_LAT, D_MODEL), F32))
        out_specs = (
            pl.BlockSpec((TM_COMB, D_MODEL), lambda i, *_: (jnp.minimum(i, CTX_TILES_COMB - 1), 0)),
            pl.BlockSpec((TM_COMB, D_MODEL), lambda i, *_: (jnp.maximum(i - CTX_TILES_COMB, 0), 0)),
        )
    else:
        out_shape = jax.ShapeDtypeStruct((T_ALL, D_MODEL), F32)
        out_specs = pl.BlockSpec((TM_COMB, D_MODEL), lambda i, *_: (i, 0))
    grid_spec = pltpu.PrefetchScalarGridSpec(
        num_scalar_prefetch=1,
        grid=(nblk,),
        in_specs=[
            pl.BlockSpec((None, 1, TM_COMB * TOP_K), lambda i, *_: (i, 0, 0), memory_space=pltpu.SMEM),
            pl.BlockSpec(memory_space=pl.ANY),
            pl.BlockSpec((TM_COMB, LANES), lambda i, *_: (i, 0)),
            pl.BlockSpec((TM_COMB, D_MODEL), lambda i, *_: (i, 0)),
            _mod_spec(layer, 5, TM_COMB),
            pl.BlockSpec((None, None, 1, D_MODEL), lambda i, *_: (layer, 1, 0, 0)),
            pl.BlockSpec((None, None, 1, D_MODEL), lambda i, *_: (layer, 1, 0, 0)),
        ],
        out_specs=out_specs,
        scratch_shapes=[pltpu.VMEM((TOP_K, TM_COMB, D_MODEL), F32), pltpu.SemaphoreType.DMA((PARTS_COMB,))],
    )
    return pl.pallas_call(
        functools.partial(_combine_kernel, split=split),
        out_shape=out_shape,
        grid_spec=grid_spec,
        compiler_params=_cparams(1),
        name="moe_combine",
    )(tile_start, codes.reshape(nblk, 1, TM_COMB * TOP_K), outs, gw, x1, mods,
      ln_w.reshape(DEPTH, 2, 1, D_MODEL), ln_b.reshape(DEPTH, 2, 1, D_MODEL))


def _routing_tables(counts_f):
    counts = counts_f[0, :N_EXPERTS].astype(jnp.int32)
    ntiles = (counts + TM_MOE - 1) // TM_MOE
    tile_end = jnp.cumsum(ntiles)
    tile_start = tile_end - ntiles
    nchunks = (ntiles + TPC - 1) // TPC
    chunk_end = jnp.cumsum(nchunks)
    total = chunk_end[-1]
    cidx = jnp.arange(NC_MAX, dtype=jnp.int32)
    last = jnp.maximum(total - 1, 0)
    ce = jnp.sum((chunk_end[None, :] <= jnp.minimum(cidx, last)[:, None]).astype(jnp.int32), axis=1)
    ce = jnp.clip(ce, 0, N_EXPERTS - 1)
    kk = cidx - (chunk_end - nchunks)[ce]
    cs = tile_start[ce] + kk * TPC
    cn = jnp.where(cidx < total, jnp.clip(ntiles[ce] - kk * TPC, 0, TPC), 0)
    cs = jnp.where(cidx < total, cs, 0)
    meta = jnp.stack([total, tile_end[-1]]).astype(jnp.int32)
    tables = (ce.astype(jnp.int32), cs.astype(jnp.int32), cn.astype(jnp.int32), meta)
    return tile_start.astype(jnp.int32), tile_end.astype(jnp.int32), tables


def kernel(x_prompt, x_sample, c, cache_na_k, cache_na_v, state_mlstm_c, state_mlstm_n, state_mlstm_m,
           c_ctx, w_mod, b_mod, w_in, b_gate, mlstm_norm_w, na_rpb, w_out, ln_w, ln_b,
           router_w, router_b, expert_w1, expert_b1, expert_w2, expert_b2):
    g0 = 4 * W_M
    w_main = jnp.concatenate([w_in[:, :, :g0], w_in[:, :, g0 + N_GATES:]], axis=-1).astype(BF16)
    wg = w_in[:, :, g0:g0 + N_GATES].reshape(DEPTH, D_MODEL, 4, H_M).transpose(0, 1, 3, 2)
    w_gate = jnp.pad(wg, ((0, 0), (0, 0), (0, 0), (0, LANES - 4))).reshape(DEPTH, D_MODEL, GATE_W).astype(BF16)
    bg = b_gate.reshape(DEPTH, 4, H_M).transpose(0, 2, 1)
    bg = jnp.pad(bg, ((0, 0), (0, 0), (0, LANES - 4))).reshape(DEPTH, 1, GATE_W)
    w_out_b = w_out.astype(BF16)
    rw = jnp.pad(router_w, ((0, 0), (0, 0), (0, LANES - N_EXPERTS)))
    rb = jnp.pad(router_b, ((0, 0), (0, LANES - N_EXPERTS)), constant_values=NEG).reshape(DEPTH, 1, LANES)

    cvec = jnp.concatenate([c_ctx[None, :], c, jnp.zeros((MOD_ROWS - 1 - DEC_BATCH, D_MODEL), F32)], axis=0)
    mods = _adaln(cvec, w_mod, b_mod).reshape(DEPTH * MOD_ROWS * 6, 1, D_MODEL)

    x = jnp.concatenate([x_prompt.reshape(T_CTX, D_MODEL), x_sample.reshape(T_LAT, D_MODEL)], axis=0)
    ks_l, vs_l, cs_l, ns_l, ms_l = [], [], [], [], []
    for l in range(DEPTH):
        proj, gates, k_ctx, v_ctx = _in_proj(l, x, mods, w_main, w_gate, bg)
        m_ctx, st_c, st_n, st_m = _mlstm(l, proj, gates, mlstm_norm_w, latent=False)
        (m_lat,) = _mlstm(l, proj, gates, mlstm_norm_w, latent=True,
                          state=(state_mlstm_c, state_mlstm_n, state_mlstm_m))
        a_ctx = _attention(l, proj, latent=False)
        a_lat = _attention(l, proj, latent=True, cache_k=cache_na_k, cache_v=cache_na_v, rpb=na_rpb)
        x1, h2p, top_i, top_g = _out_proj(l, m_ctx, m_lat, a_ctx, a_lat, w_out_b, x, mods, ln_w, ln_b, rw, rb)
        code, counts = _route(top_i)
        tile_start, tile_end, tables = _routing_tables(counts)
        codes = code[:, :TOP_K].reshape(-1)
        xs = _scatter_rows(h2p, codes, tile_start, tile_end, tables[3])
        outs = _moe_experts(l, xs, tables, expert_w1, expert_b1, expert_w2, expert_b2)
        x = _combine(l, outs, codes, tile_start, top_g, x1, mods, ln_w, ln_b, split=(l == DEPTH - 1))

        ks_l.append(k_ctx.reshape(BATCH, SEQ, H_NA, D_NA))
        vs_l.append(v_ctx.reshape(BATCH, SEQ, H_NA, D_NA))
        cs_l.append(st_c)
        ns_l.append(st_n.reshape(BATCH, 2, H_M, DK_M))
        ms_l.append(st_m[:, :, :, 0, 0])

    y_prompt = x[0].reshape(BATCH, SEQ, D_MODEL)
    y_sample = x[1].reshape(DEC_BATCH, DEC_SEQ, D_MODEL)
    return (y_prompt, y_sample, jnp.stack(ks_l, axis=1), jnp.stack(vs_l, axis=1),
            jnp.stack(cs_l, axis=1), jnp.stack(ns_l, axis=1), jnp.stack(ms_l, axis=1))
```

```python
import functools

import numpy as np
import jax
import jax.numpy as jnp
from jax import lax
from jax.experimental import pallas as pl
from jax.experimental.pallas import tpu as pltpu

F32 = jnp.float32
BF16 = jnp.bfloat16

D_MODEL = 2048
BATCH, SEQ = 16, 256
DEC_BATCH, DEC_SEQ = 8, 1024
DEPTH = 2
PAST_LEN = 512
GRID_W = 64
H_M, DK_M, DV_M = 4, 256, 256
H_NA, D_NA = 8, 128
MAX_KH, KW = 8, 16
N_EXPERTS, TOP_K, D_FF = 32, 4, 2048
SWIGLU_LIMIT, SWIGLU_ALPHA = 7.0, 1.702
ROPE_BASE = 10000.0
CHUNK = 128
N_GATES = 4 * H_M
W_M = H_M * DV_M
W_NA = H_NA * D_NA
ALPHA = (2 * DEPTH) ** 0.25
NEG = -1e30
LN_EPS = 1e-5
HN_EPS = 1e-6

T_CTX = BATCH * SEQ
T_LAT = DEC_BATCH * DEC_SEQ
T_ALL = T_CTX + T_LAT
N_MAIN = 4 * W_M + 3 * W_NA
N_ASSIGN = T_ALL * TOP_K

LANES = 128
VMEM_LIMIT = 56 * 1024 * 1024

TM_IN = 1024
TN_IN = 512
TM_OUT = 256
TM_ROUTE = 256
TM_MOE = 256
CH_MOE = 2048
TPC = CH_MOE // TM_MOE
TF = 256
NJ = D_FF // TF
ROWS_SORTED = N_ASSIGN + N_EXPERTS * TM_MOE
NT_SORTED = ROWS_SORTED // TM_MOE
NC_MAX = N_EXPERTS + -(-NT_SORTED // TPC)
TM_COMB = 128
D_PACK = D_MODEL // 2
RANK_BITS = 16
assert N_ASSIGN < (1 << RANK_BITS)


def _cparams(n_axes, vmem=VMEM_LIMIT):
    return pltpu.CompilerParams(dimension_semantics=("arbitrary",) * n_axes, vmem_limit_bytes=vmem)


def _dot(a, b):
    return jnp.dot(a, b, preferred_element_type=F32)


def _split2(x):
    hi = x.astype(BF16)
    lo = (x - hi.astype(F32)).astype(BF16)
    return hi, lo


def _dot_f32(a, b):
    a1, a2 = _split2(a)
    b1, b2 = _split2(b)
    return _dot(a1, b1) + (_dot(a1, b2) + _dot(a2, b1))


def _pack_bf16_pairs(x):
    n = x.shape[1] // 2
    bits = lax.bitcast_convert_type(x.astype(BF16).astype(F32), jnp.uint32)
    return (bits[:, :n] >> 16) | (bits[:, n:] & jnp.uint32(0xFFFF0000))


def _unpack_bf16_pairs(words):
    lo = lax.bitcast_convert_type(words << 16, F32)
    hi = lax.bitcast_convert_type(words & jnp.uint32(0xFFFF0000), F32)
    return lo, hi


def _layer_norm(z, w, b):
    mu = jnp.mean(z, axis=-1, keepdims=True)
    zc = z - mu
    var = jnp.mean(zc * zc, axis=-1, keepdims=True)
    return zc * lax.rsqrt(var + LN_EPS) * w + b


MOD_ROWS = 16
TN_MOD = 512


def _adaln_kernel(c_ref, w_ref, b_ref, o_ref):
    c = c_ref[...]
    s = c * jax.nn.sigmoid(c)
    o_ref[...] = _dot_f32(s, w_ref[...]) + b_ref[...]


def _adaln(cvec, w_mod, b_mod):
    n = 6 * D_MODEL
    return pl.pallas_call(
        _adaln_kernel,
        out_shape=jax.ShapeDtypeStruct((DEPTH, MOD_ROWS, n), F32),
        grid=(DEPTH, n // TN_MOD),
        in_specs=[
            pl.BlockSpec((MOD_ROWS, D_MODEL), lambda l, j: (0, 0)),
            pl.BlockSpec((None, D_MODEL, TN_MOD), lambda l, j: (l, 0, j)),
            pl.BlockSpec((None, 1, TN_MOD), lambda l, j: (l, 0, j)),
        ],
        out_specs=pl.BlockSpec((None, MOD_ROWS, TN_MOD), lambda l, j: (l, 0, j)),
        compiler_params=_cparams(2),
        name="adaln",
    )(cvec, w_mod, b_mod.reshape(DEPTH, 1, n))


def _mod_spec(layer, k, rows_per_tile):
    ctx_tiles = T_CTX // rows_per_tile
    tiles_per_lat = DEC_SEQ // rows_per_tile

    def index(i, *_):
        r = jnp.where(i < ctx_tiles, 0, 1 + (i - ctx_tiles) // tiles_per_lat)
        return ((layer * MOD_ROWS + r) * 6 + k, 0, 0)

    return pl.BlockSpec((None, 1, D_MODEL), index)


GATE_W = H_M * LANES


CTX_TILES_IN = T_CTX // TM_IN
SEQ_PER_TILE = TM_IN // SEQ
J_NK = (4 * W_M + W_NA) // TN_IN
J_NV = (4 * W_M + 2 * W_NA) // TN_IN
KV_TILES = W_NA // TN_IN


def _inproj_kernel(x_ref, sh_ref, sc_ref, w_ref, wg_ref, bg_ref, o_ref, g_ref, ko_ref, vo_ref, h_scr):
    i = pl.program_id(0)
    j = pl.program_id(1)

    @pl.when(j == 0)
    def _():
        h = (x_ref[...] * (1.0 + sc_ref[...]) + sh_ref[...]).astype(BF16)
        h_scr[...] = h
        g_ref[...] = _dot(h, wg_ref[...]) + bg_ref[...]

    out = _dot(h_scr[...], w_ref[...])
    o_ref[...] = out
    is_ctx = i < CTX_TILES_IN

    @pl.when(is_ctx & (j >= J_NK) & (j < J_NK + KV_TILES))
    def _():
        ko_ref[...] = out.reshape(SEQ_PER_TILE, SEQ, TN_IN)

    @pl.when(is_ctx & (j >= J_NV))
    def _():
        vo_ref[...] = out.reshape(SEQ_PER_TILE, SEQ, TN_IN)


def _kv_spec(j_first):
    def index(i, j):
        jt = jnp.clip(j - j_first, 0, KV_TILES - 1)
        ctx = i < CTX_TILES_IN
        return (jnp.where(ctx, i, CTX_TILES_IN - 1), 0, jnp.where(ctx, jt, KV_TILES - 1))

    return pl.BlockSpec((SEQ_PER_TILE, SEQ, TN_IN), index)


def _in_proj(layer, x, mods, w_main, w_gate, b_gate):
    kv_shape = jax.ShapeDtypeStruct((BATCH, SEQ, W_NA), F32)
    return pl.pallas_call(
        _inproj_kernel,
        out_shape=(jax.ShapeDtypeStruct((T_ALL, N_MAIN), F32),
                   jax.ShapeDtypeStruct((T_ALL, GATE_W), F32), kv_shape, kv_shape),
        grid=(T_ALL // TM_IN, N_MAIN // TN_IN),
        in_specs=[
            pl.BlockSpec((TM_IN, D_MODEL), lambda i, j: (i, 0)),
            _mod_spec(layer, 0, TM_IN),
            _mod_spec(layer, 1, TM_IN),
            pl.BlockSpec((None, D_MODEL, TN_IN), lambda i, j: (layer, 0, j)),
            pl.BlockSpec((None, D_MODEL, GATE_W), lambda i, j: (layer, 0, 0)),
            pl.BlockSpec((None, 1, GATE_W), lambda i, j: (layer, 0, 0)),
        ],
        out_specs=(pl.BlockSpec((TM_IN, TN_IN), lambda i, j: (i, j)),
                   pl.BlockSpec((TM_IN, GATE_W), lambda i, j: (i, 0)),
                   _kv_spec(J_NK), _kv_spec(J_NV)),
        scratch_shapes=[pltpu.VMEM((TM_IN, D_MODEL), BF16)],
        compiler_params=_cparams(2),
        name="in_proj",
    )(x, mods, mods, w_main, w_gate, b_gate)


HPB = 2


def _log_sigmoid(x):
    return jnp.minimum(x, 0.0) - jnp.log(1.0 + jnp.exp(-jnp.abs(x)))


def _mlstm_kernel(*refs, seq, use_rope, has_init, emit_state, layer):
    it = iter(refs)
    q_ref, k_ref, v_ref, mo_ref, g_ref, nw_ref = (next(it) for _ in range(6))
    if use_rope:
        cos_ref, sin_ref = next(it), next(it)
    if has_init:
        c0_ref, n0_ref, m0_ref = next(it), next(it), next(it)
    out_ref = next(it)
    if emit_state:
        co_ref, no_ref, mso_ref = next(it), next(it), next(it)
    qs, ks, hs, c_scr = next(it), next(it), next(it), next(it)

    nc = seq // CHUNK
    b_idx = pl.program_id(0)
    h_idx = pl.program_id(1)

    def prep(j, carry):
        r0 = pl.multiple_of(j * CHUNK, CHUNK)
        rows = pl.ds(r0, CHUNK)
        q = q_ref[rows, :]
        k = k_ref[rows, :]
        if use_rope:
            cos = cos_ref[rows, :]
            sin = sin_ref[rows, :]

            def rope(x):
                halves = []
                for p in range(HPB * DK_M // LANES):
                    xp = x[:, p * LANES:(p + 1) * LANES]
                    t = (p % (DK_M // LANES)) * LANES
                    halves.append(xp * cos[:, t:t + LANES] + pltpu.roll(xp, LANES // 2, axis=1) * sin[:, t:t + LANES])
                return jnp.concatenate(halves, axis=1)

            q = rope(q)
            k = rope(k)
        qs[rows, :] = (q * (DK_M ** -0.5)).astype(BF16)
        ks[rows, :] = k.astype(BF16)
        return carry

    lax.fori_loop(0, nc, prep, 0)

    row_i = lax.broadcasted_iota(jnp.int32, (CHUNK, CHUNK), 0)
    col_i = lax.broadcasted_iota(jnp.int32, (CHUNK, CHUNK), 1)

    def chunk_rows(j):
        return pl.ds(pl.multiple_of(j * CHUNK, CHUNK), CHUNK)

    def head_cols(hh):
        return slice(hh * DK_M, (hh + 1) * DK_M)

    chains = [(hh, d) for hh in range(HPB) for d in range(2)]
    init = []
    for hh, d in chains:
        if has_init:
            c_scr[d, hh] = c0_ref[d, hh]
            m0 = m0_ref[((b_idx * DEPTH + layer) * 2 + d) * H_M + h_idx * HPB + hh]
            init += [n0_ref[d, hh], jnp.full((1, 1), m0, F32)]
        else:
            c_scr[d, hh] = jnp.zeros((DK_M, DV_M), F32)
            init += [jnp.zeros((1, DK_M), F32), jnp.full((1, 1), NEG, F32)]

    def chunk_step(hh, d, j, n, m):
        mask = (col_i <= row_i) if d == 0 else (col_i >= row_i)
        tri = mask.astype(BF16)
        lane_i, lane_b = 2 * d, 2 * d + 1
        end_row = CHUNK - 1 if d == 0 else 0
        rows = chunk_rows(j)
        hcols = head_cols(hh)
        gates = g_ref[rows, hh * LANES:(hh + 1) * LANES]
        lf = _log_sigmoid(gates)
        lf1, lf2 = _split2(lf)
        lf3 = (lf - lf1.astype(F32) - lf2.astype(F32)).astype(BF16)
        bc = _dot(tri, lf1) + (_dot(tri, lf2) + _dot(tri, lf3))
        gates_t = gates.T
        bc_t = bc.T
        b_col = bc[:, lane_b:lane_b + 1]
        ig_col = gates[:, lane_i:lane_i + 1]
        b_row = bc_t[lane_b:lane_b + 1, :]
        ig_row = gates_t[lane_i:lane_i + 1, :]
        b_end = bc[end_row:end_row + 1, lane_b:lane_b + 1]

        dlog = jnp.where(mask, (b_col - b_row) + ig_row, NEG)
        g = b_col + m
        mt = jnp.maximum(g, jnp.max(dlog, axis=1, keepdims=True))
        qc = qs[rows, hcols]
        kc = ks[rows, hcols]
        vc = v_ref[rows, hcols].astype(BF16)
        s = lax.dot_general(qc, kc, (((1,), (1,)), ((), ())), preferred_element_type=F32)
        s = s * jnp.exp(dlog - mt)
        sg = jnp.exp(g - mt)
        c_old = c_scr[d, hh]
        num = _dot(s.astype(BF16), vc) + sg * _dot(qc, c_old.astype(BF16))
        den = (jnp.sum(s, axis=1, keepdims=True)
               + sg * jnp.sum(qc.astype(F32) * n, axis=1, keepdims=True))
        h = num / jnp.maximum(jnp.abs(den), jnp.exp(-mt))

        wlog = (b_end - b_col) + ig_col
        m_new = jnp.maximum(b_end + m, jnp.max(wlog, axis=0, keepdims=True))
        ws = jnp.exp(wlog - m_new)
        sc = jnp.exp(b_end + m - m_new)
        kw = kc.astype(F32) * ws
        c_scr[d, hh] = sc * c_old + _dot(kw.T.astype(BF16), vc)
        n_new = sc * n + jnp.sum(kw, axis=0, keepdims=True)
        return h, n_new, m_new

    def finish(hh, j, ht):
        rows = chunk_rows(j)
        hcols = head_cols(hh)
        mu = jnp.mean(ht, axis=1, keepdims=True)
        hc = ht - mu
        var = jnp.mean(hc * hc, axis=1, keepdims=True)
        hn = hc * lax.rsqrt(var + HN_EPS) * nw_ref[:, hcols]
        out_ref[rows, hcols] = (jax.nn.sigmoid(mo_ref[rows, hcols]) * hn).astype(out_ref.dtype)

    def advance(jj, carry, complete):
        new = []
        for idx, (hh, d) in enumerate(chains):
            j = jj if d == 0 else nc - 1 - jj
            h, n, m = chunk_step(hh, d, j, carry[2 * idx], carry[2 * idx + 1])
            new += [n, m]
            if complete:
                finish(hh, j, hs[chunk_rows(j), head_cols(hh)] + h)
            else:
                hs[chunk_rows(j), head_cols(hh)] = h
        return tuple(new)

    carry = lax.fori_loop(0, nc // 2, functools.partial(advance, complete=False), tuple(init))
    carry = lax.fori_loop(nc // 2, nc, functools.partial(advance, complete=True), carry)
    if emit_state:
        for idx, (hh, d) in enumerate(chains):
            co_ref[d, hh] = c_scr[d, hh]
            no_ref[d, hh] = carry[2 * idx]
            mso_ref[d, hh] = jnp.broadcast_to(carry[2 * idx + 1], (1, LANES))


def _rope_tables(seq):
    pos = np.arange(seq)
    half = DK_M // 2
    nf = half // 2
    inv = ROPE_BASE ** (-np.arange(nf, dtype=np.float32) / nf)
    cos_parts, sin_parts = [], []
    for p in ((pos // GRID_W).astype(np.float32), (pos % GRID_W).astype(np.float32)):
        ang = (p[:, None] * inv).astype(np.float32)
        cos_parts += [np.cos(ang), np.cos(ang)]
        sin_parts += [-np.sin(ang), np.sin(ang)]
    return (jnp.asarray(np.concatenate(cos_parts, axis=1), F32),
            jnp.asarray(np.concatenate(sin_parts, axis=1), F32))


def _mlstm(layer, proj, gates, norm_w, *, latent, state=None):
    if latent:
        nb, seq, rb0 = DEC_BATCH, DEC_SEQ, T_CTX // DEC_SEQ
    else:
        nb, seq, rb0 = BATCH, SEQ, 0
    cb = W_M // (HPB * DK_M)
    wq = HPB * DK_M
    in_specs = [
        pl.BlockSpec((seq, wq), lambda b, h: (rb0 + b, h)),
        pl.BlockSpec((seq, wq), lambda b, h: (rb0 + b, cb + h)),
        pl.BlockSpec((seq, wq), lambda b, h: (rb0 + b, 2 * cb + h)),
        pl.BlockSpec((seq, wq), lambda b, h: (rb0 + b, 3 * cb + h)),
        pl.BlockSpec((seq, HPB * LANES), lambda b, h: (rb0 + b, h)),
        pl.BlockSpec((None, 1, wq), lambda b, h: (layer, 0, h)),
    ]
    args = [proj, proj, proj, proj, gates, norm_w.reshape(DEPTH, 1, W_M)]
    if latent:
        cos, sin = _rope_tables(seq)
        in_specs += [pl.BlockSpec((seq, DK_M), lambda b, h: (0, 0))] * 2
        args += [cos, sin]
        st_c, st_n, st_m = state
        in_specs += [
            pl.BlockSpec((None, None, 2, HPB, DK_M, DV_M), lambda b, h: (b, layer, 0, h, 0, 0)),
            pl.BlockSpec((None, None, 2, HPB, 1, DK_M), lambda b, h: (b, layer, 0, h, 0, 0)),
            pl.BlockSpec(memory_space=pltpu.SMEM),
        ]
        args += [st_c, st_n.reshape(DEC_BATCH, DEPTH, 2, H_M, 1, DK_M), st_m.reshape(-1)]
    out_shape = [jax.ShapeDtypeStruct((nb * seq, W_M), BF16)]
    out_specs = [pl.BlockSpec((seq, wq), lambda b, h: (b, h))]
    if not latent:
        out_shape += [
            jax.ShapeDtypeStruct((nb, 2, H_M, DK_M, DV_M), F32),
            jax.ShapeDtypeStruct((nb, 2, H_M, 1, DK_M), F32),
            jax.ShapeDtypeStruct((nb, 2, H_M, 1, LANES), F32),
        ]
        out_specs += [
            pl.BlockSpec((None, 2, HPB, DK_M, DV_M), lambda b, h: (b, 0, h, 0, 0)),
            pl.BlockSpec((None, 2, HPB, 1, DK_M), lambda b, h: (b, 0, h, 0, 0)),
            pl.BlockSpec((None, 2, HPB, 1, LANES), lambda b, h: (b, 0, h, 0, 0)),
        ]
    kern = functools.partial(_mlstm_kernel, seq=seq, use_rope=latent, has_init=latent,
                             emit_state=not latent, layer=layer)
    return pl.pallas_call(
        kern,
        out_shape=tuple(out_shape),
        grid=(nb, H_M // HPB),
        in_specs=in_specs,
        out_specs=tuple(out_specs),
        scratch_shapes=[
            pltpu.VMEM((seq, wq), BF16),
            pltpu.VMEM((seq, wq), BF16),
            pltpu.VMEM((seq, wq), F32),
            pltpu.VMEM((2, HPB, DK_M, DV_M), F32),
        ],
        compiler_params=_cparams(2),
        name="mlstm_lat" if latent else "mlstm_ctx",
    )(*args)


RPB_ROWS = 2 * MAX_KH - 1
RPB_COLS = 2 * KW - 1
Q_TILE = 256
LAT_ROWS = DEC_SEQ // GRID_W
KH = min(MAX_KH, LAT_ROWS)


def _attn_ctx_kernel(q_ref, k_ref, v_ref, o_ref):
    scale = D_NA ** -0.5
    for h in range(H_NA):
        cols = slice(h * D_NA, (h + 1) * D_NA)
        q = q_ref[:, cols].astype(BF16)
        k = k_ref[:, cols].astype(BF16)
        v = v_ref[:, cols].astype(BF16)
        s = lax.dot_general(q, k, (((1,), (1,)), ((), ())), preferred_element_type=F32) * scale
        m = jnp.max(s, axis=1, keepdims=True)
        p = jnp.exp(s - m)
        l = jnp.sum(p, axis=1, keepdims=True)
        o_ref[:, cols] = (_dot(p.astype(BF16), v) / l).astype(o_ref.dtype)


def _attn_lat_kernel(rpb_ref, q_ref, k_ref, v_ref, kc_ref, vc_ref, o_ref, bias_scr, g_scr):
    h_idx = pl.program_id(0)
    scale = D_NA ** -0.5

    @pl.when(pl.program_id(1) == 0)
    def _build_bias():
        qc = lax.broadcasted_iota(jnp.int32, (GRID_W, LANES), 0)
        lane = lax.broadcasted_iota(jnp.int32, (GRID_W, LANES), 1)
        kc = lane & (GRID_W - 1)
        upper = lane >= GRID_W
        cidx = jnp.clip(kc - qc, -(KW - 1), KW - 1) + (KW - 1)
        cs = jnp.clip(qc - KW // 2, 0, GRID_W - KW)
        col_ok = (kc >= cs) & (kc < cs + KW)
        base = h_idx * (RPB_ROWS * RPB_COLS)
        for dr0 in range(-KH, KH):
            acc = jnp.zeros((GRID_W, LANES), F32)
            for c in range(RPB_COLS):
                lo = rpb_ref[base + (dr0 + MAX_KH - 1) * RPB_COLS + c] if -MAX_KH < dr0 < MAX_KH else 0.0
                hi = rpb_ref[base + (dr0 + MAX_KH) * RPB_COLS + c] if -MAX_KH < dr0 + 1 < MAX_KH else 0.0
                acc = jnp.where(cidx == c, jnp.where(upper, hi, lo), acc)
            g_scr[dr0 + KH] = jnp.where(col_ok, acc, NEG)
        neg_tile = jnp.full((GRID_W, LANES), NEG, F32)
        for qr in range(LAT_ROWS):
            rs = min(max(qr - KH // 2, 0), LAT_ROWS - KH)
            for t in range(LAT_ROWS // 2):
                ok0 = rs <= 2 * t < rs + KH
                ok1 = rs <= 2 * t + 1 < rs + KH
                if ok0 or ok1:
                    tile = g_scr[2 * t - qr + KH]
                    if not ok1:
                        tile = jnp.where(upper, NEG, tile)
                    if not ok0:
                        tile = jnp.where(upper, tile, NEG)
                else:
                    tile = neg_tile
                bias_scr[qr * GRID_W:(qr + 1) * GRID_W, t * LANES:(t + 1) * LANES] = tile

    k = k_ref[...].astype(BF16)
    v = v_ref[...].astype(BF16)
    kc = kc_ref[...].astype(BF16)
    vc = vc_ref[...].astype(BF16)
    dn = (((1,), (1,)), ((), ()))
    for qt in range(DEC_SEQ // Q_TILE):
        rows = slice(qt * Q_TILE, (qt + 1) * Q_TILE)
        q = q_ref[rows, :].astype(BF16)
        s_loc = lax.dot_general(q, k, dn, preferred_element_type=F32) * scale + bias_scr[rows, :]
        s_ctx = lax.dot_general(q, kc, dn, preferred_element_type=F32) * scale
        m = jnp.maximum(jnp.max(s_loc, axis=1, keepdims=True), jnp.max(s_ctx, axis=1, keepdims=True))
        p_loc = jnp.exp(s_loc - m)
        p_ctx = jnp.exp(s_ctx - m)
        l = jnp.sum(p_loc, axis=1, keepdims=True) + jnp.sum(p_ctx, axis=1, keepdims=True)
        o = _dot(p_loc.astype(BF16), v) + _dot(p_ctx.astype(BF16), vc)
        o_ref[rows, :] = (o / l).astype(o_ref.dtype)


def _attention(layer, proj, *, latent, cache_k=None, cache_v=None, rpb=None):
    qb0 = 4 * W_M // D_NA
    if not latent:
        return pl.pallas_call(
            _attn_ctx_kernel,
            out_shape=jax.ShapeDtypeStruct((T_CTX, W_NA), BF16),
            grid=(BATCH,),
            in_specs=[
                pl.BlockSpec((SEQ, W_NA), lambda b: (b, qb0 // H_NA)),
                pl.BlockSpec((SEQ, W_NA), lambda b: (b, qb0 // H_NA + 1)),
                pl.BlockSpec((SEQ, W_NA), lambda b: (b, qb0 // H_NA + 2)),
            ],
            out_specs=pl.BlockSpec((SEQ, W_NA), lambda b: (b, 0)),
            compiler_params=_cparams(1),
            name="attn_ctx",
        )(proj, proj, proj)
    rb0 = T_CTX // DEC_SEQ
    ck = cache_k.reshape(DEC_BATCH, DEPTH, PAST_LEN, W_NA)
    cv = cache_v.reshape(DEC_BATCH, DEPTH, PAST_LEN, W_NA)
    grid_spec = pltpu.PrefetchScalarGridSpec(
        num_scalar_prefetch=1,
        grid=(H_NA, DEC_BATCH),
        in_specs=[
            pl.BlockSpec((DEC_SEQ, D_NA), lambda h, b, r: (rb0 + b, qb0 + h)),
            pl.BlockSpec((DEC_SEQ, D_NA), lambda h, b, r: (rb0 + b, qb0 + H_NA + h)),
            pl.BlockSpec((DEC_SEQ, D_NA), lambda h, b, r: (rb0 + b, qb0 + 2 * H_NA + h)),
            pl.BlockSpec((None, None, PAST_LEN, D_NA), lambda h, b, r: (b, layer, 0, h)),
            pl.BlockSpec((None, None, PAST_LEN, D_NA), lambda h, b, r: (b, layer, 0, h)),
        ],
        out_specs=pl.BlockSpec((DEC_SEQ, D_NA), lambda h, b, r: (b, h)),
        scratch_shapes=[
            pltpu.VMEM((DEC_SEQ, DEC_SEQ), F32),
            pltpu.VMEM((2 * KH, GRID_W, LANES), F32),
        ],
    )
    return pl.pallas_call(
        _attn_lat_kernel,
        out_shape=jax.ShapeDtypeStruct((T_LAT, W_NA), BF16),
        grid_spec=grid_spec,
        compiler_params=_cparams(2),
        name="attn_lat",
    )(rpb[layer].reshape(-1), proj, proj, proj, ck, cv)


def _outproj_kernel(mc_ref, ml_ref, ac_ref, al_ref, w_ref, x_ref, ga_ref, shf_ref, scf_ref,
                    lnw_ref, lnb_ref, rw_ref, rb_ref, x1_ref, h2_ref, ti_ref, tg_ref):
    is_ctx = pl.program_id(0) < T_CTX // TM_OUT
    m = jnp.where(is_ctx, mc_ref[...], ml_ref[...])
    a = jnp.where(is_ctx, ac_ref[...], al_ref[...])
    y = _dot(m, w_ref[0:W_M, :]) + _dot(a, w_ref[W_M:W_M + W_NA, :])
    x1 = _layer_norm(ALPHA * x_ref[...] + ga_ref[...] * y, lnw_ref[...], lnb_ref[...])
    x1_ref[...] = x1
    h2 = x1 * (1.0 + scf_ref[...]) + shf_ref[...]
    h2_ref[...] = _pack_bf16_pairs(h2)
    logits = _dot_f32(h2, rw_ref[...]) + rb_ref[...]
    lane = lax.broadcasted_iota(jnp.int32, logits.shape, 1)
    vals = logits
    top_v, top_i = [], []
    for _ in range(TOP_K):
        mx = jnp.max(vals, axis=1, keepdims=True)
        ix = jnp.min(jnp.where(vals == mx, lane, LANES), axis=1, keepdims=True)
        top_v.append(mx)
        top_i.append(ix)
        vals = jnp.where(lane == ix, -jnp.inf, vals)
    ex = [jnp.exp(v - top_v[0]) for v in top_v]
    den = ex[0] + ex[1] + ex[2] + ex[3]
    ti = jnp.zeros(logits.shape, jnp.int32)
    tg = jnp.zeros(logits.shape, F32)
    for kk in range(TOP_K):
        ti = jnp.where(lane == kk, top_i[kk], ti)
        tg = jnp.where(lane == kk, ex[kk] / den, tg)
    ti_ref[...] = ti
    tg_ref[...] = tg


def _out_proj(layer, m_ctx, m_lat, a_ctx, a_lat, w_out, x, mods, ln_w, ln_b, router_w, router_b):
    nctx = T_CTX // TM_OUT
    ctx_spec = lambda w: pl.BlockSpec((TM_OUT, w), lambda i: (jnp.minimum(i, nctx - 1), 0))
    lat_spec = lambda w: pl.BlockSpec((TM_OUT, w), lambda i: (jnp.maximum(i - nctx, 0), 0))
    row_spec = lambda w: pl.BlockSpec((TM_OUT, w), lambda i: (i, 0))
    return pl.pallas_call(
        _outproj_kernel,
        out_shape=(jax.ShapeDtypeStruct((T_ALL, D_MODEL), F32),
                   jax.ShapeDtypeStruct((T_ALL, D_PACK), jnp.uint32),
                   jax.ShapeDtypeStruct((T_ALL, LANES), jnp.int32),
                   jax.ShapeDtypeStruct((T_ALL, LANES), F32)),
        grid=(T_ALL // TM_OUT,),
        in_specs=[
            ctx_spec(W_M), lat_spec(W_M), ctx_spec(W_NA), lat_spec(W_NA),
            pl.BlockSpec((None, W_M + W_NA, D_MODEL), lambda i: (layer, 0, 0)),
            row_spec(D_MODEL),
            _mod_spec(layer, 2, TM_OUT), _mod_spec(layer, 3, TM_OUT), _mod_spec(layer, 4, TM_OUT),
            pl.BlockSpec((None, None, 1, D_MODEL), lambda i: (layer, 0, 0, 0)),
            pl.BlockSpec((None, None, 1, D_MODEL), lambda i: (layer, 0, 0, 0)),
            pl.BlockSpec((None, D_MODEL, LANES), lambda i: (layer, 0, 0)),
            pl.BlockSpec((None, 1, LANES), lambda i: (layer, 0, 0)),
        ],
        out_specs=(row_spec(D_MODEL), row_spec(D_PACK), row_spec(LANES), row_spec(LANES)),
        compiler_params=_cparams(1),
        name="out_proj",
    )(m_ctx, m_lat, a_ctx, a_lat, w_out, x, mods, mods, mods,
      ln_w.reshape(DEPTH, 2, 1, D_MODEL), ln_b.reshape(DEPTH, 2, 1, D_MODEL), router_w, router_b)


def _route_kernel(ti_ref, code_ref, cnt_ref, carry):
    @pl.when(pl.program_id(0) == 0)
    def _():
        carry[...] = jnp.zeros((1, LANES), F32)

    ti = ti_ref[...]
    lane = lax.broadcasted_iota(jnp.int32, (TM_ROUTE, LANES), 1)
    hits = [lane == ti[:, kk:kk + 1] for kk in range(TOP_K)]
    cnt = hits[0].astype(F32)
    for kk in range(1, TOP_K):
        cnt = cnt + hits[kk].astype(F32)
    row_i = lax.broadcasted_iota(jnp.int32, (TM_ROUTE, TM_ROUTE), 0)
    col_i = lax.broadcasted_iota(jnp.int32, (TM_ROUTE, TM_ROUTE), 1)
    before = _dot((col_i < row_i).astype(BF16), cnt.astype(BF16)) + carry[...]
    code = jnp.zeros((TM_ROUTE, LANES), jnp.int32)
    for kk in range(TOP_K):
        rank = jnp.sum(jnp.where(hits[kk], before, 0.0), axis=1, keepdims=True).astype(jnp.int32)
        code = jnp.where(lane == kk, (ti[:, kk:kk + 1] << RANK_BITS) | rank, code)
    code_ref[...] = code
    carry[...] = carry[...] + jnp.sum(cnt, axis=0, keepdims=True)
    cnt_ref[...] = carry[...]


def _route(top_i):
    return pl.pallas_call(
        _route_kernel,
        out_shape=(jax.ShapeDtypeStruct((T_ALL, LANES), jnp.int32),
                   jax.ShapeDtypeStruct((1, LANES), F32)),
        grid=(T_ALL // TM_ROUTE,),
        in_specs=[pl.BlockSpec((TM_ROUTE, LANES), lambda i: (i, 0))],
        out_specs=(pl.BlockSpec((TM_ROUTE, LANES), lambda i: (i, 0)),
                   pl.BlockSpec((1, LANES), lambda i: (0, 0))),
        scratch_shapes=[pltpu.VMEM((1, LANES), F32)],
        compiler_params=_cparams(1),
        name="moe_route",
    )(top_i)


def _sorted_row(ts_ref, code):
    return ts_ref[code >> RANK_BITS] * TM_MOE + (code & ((1 << RANK_BITS) - 1))


def _scatter_kernel(ts_ref, te_ref, meta_ref, code_ref, src_ref, xs_ref, zero_scr, sem):
    @pl.when(pl.program_id(0) == 0)
    def _define_padding():
        zero_scr[...] = jnp.zeros((TM_MOE, D_PACK), jnp.uint32)

        def fill(t):
            dst = xs_ref.at[pl.ds(pl.multiple_of(t * TM_MOE, TM_MOE), TM_MOE), :]
            return pltpu.make_async_copy(zero_scr, dst, sem)

        def per_expert(fn):
            def body(e, carry):
                @pl.when(te_ref[e] > ts_ref[e])
                def _():
                    fn(fill(te_ref[e] - 1))
                return carry
            lax.fori_loop(0, N_EXPERTS, body, 0)

        def tail(fn):
            def body(t, carry):
                fn(fill(t))
                return carry
            lax.fori_loop(meta_ref[1], NT_SORTED, body, 0)

        per_expert(lambda cp: cp.start())
        tail(lambda cp: cp.start())
        per_expert(lambda cp: cp.wait())
        tail(lambda cp: cp.wait())

    def start(t, carry):
        for kk in range(TOP_K):
            dst = _sorted_row(ts_ref, code_ref[0, t * TOP_K + kk])
            pltpu.make_async_copy(src_ref.at[pl.ds(t, 1), :], xs_ref.at[pl.ds(dst, 1), :], sem).start(priority=kk % 2)
        return carry

    def wait(t, carry):
        for kk in range(TOP_K):
            pltpu.make_async_copy(src_ref.at[pl.ds(0, 1), :], xs_ref.at[pl.ds(0, 1), :], sem).wait()
        return carry

    lax.fori_loop(0, TM_ROUTE, start, 0, unroll=2)
    lax.fori_loop(0, TM_ROUTE, wait, 0)


def _scatter_rows(src, codes, tile_start, tile_end, meta):
    nblk = T_ALL // TM_ROUTE
    grid_spec = pltpu.PrefetchScalarGridSpec(
        num_scalar_prefetch=3,
        grid=(nblk,),
        in_specs=[
            pl.BlockSpec((None, 1, TM_ROUTE * TOP_K), lambda i, *_: (i, 0, 0), memory_space=pltpu.SMEM),
            pl.BlockSpec((TM_ROUTE, D_PACK), lambda i, *_: (i, 0)),
        ],
        out_specs=pl.BlockSpec(memory_space=pl.ANY),
        scratch_shapes=[pltpu.VMEM((TM_MOE, D_PACK), jnp.uint32), pltpu.SemaphoreType.DMA(())],
    )
    return pl.pallas_call(
        _scatter_kernel,
        out_shape=jax.ShapeDtypeStruct((ROWS_SORTED, D_PACK), jnp.uint32),
        grid_spec=grid_spec,
        compiler_params=_cparams(1),
        name="moe_scatter",
    )(tile_start, tile_end, meta, codes.reshape(nblk, 1, TM_ROUTE * TOP_K), src)


def _moe_kernel(ce_ref, cs_ref, cn_ref, nch_ref, xs_ref, w1_ref, b1_ref, w2_ref, b2_ref, sel_ref,
                out_ref, xbuf, acc, outp, w1b, w2b, stage, sem_in, sem_out):
    c = pl.program_id(0)
    j = pl.program_id(1)
    nt = cn_ref[c]
    row0 = cs_ref[c] * TM_MOE

    def tile_rows(r):
        return pl.ds(pl.multiple_of(r * TM_MOE, TM_MOE), TM_MOE)

    @pl.when(nt > 0)
    def _chunk():
        @pl.when(j == 0)
        def _load_rows():
            def fetch(r, slot):
                src = xs_ref.at[pl.ds(pl.multiple_of(row0 + r * TM_MOE, TM_MOE), TM_MOE), :]
                return pltpu.make_async_copy(src, stage.at[slot], sem_in.at[slot])

            fetch(0, 0).start()

            def load(r, carry):
                slot = r & 1
                fetch(r, slot).wait()

                @pl.when(r + 1 < nt)
                def _():
                    fetch(r + 1, 1 - slot).start()

                lo, hi = _unpack_bf16_pairs(stage[slot])
                xbuf[tile_rows(r), 0:D_PACK] = lo.astype(BF16)
                xbuf[tile_rows(r), D_PACK:D_MODEL] = hi.astype(BF16)
                acc[tile_rows(r), :] = jnp.broadcast_to(b2_ref[...], (TM_MOE, D_MODEL))
                return carry

            lax.fori_loop(0, nt, load, 0)

        w1b[...] = w1_ref[...].astype(BF16)
        w2b[...] = w2_ref[...].astype(BF16)

        def out_copy(start):
            rows = pl.ds(pl.multiple_of(start, TM_MOE), TM_MOE)
            dst = out_ref.at[pl.ds(pl.multiple_of(row0 + start, TM_MOE), TM_MOE), :]
            return pltpu.make_async_copy(outp.at[rows, :], dst, sem_out)

        def tile(start, size):
            rows = pl.ds(pl.multiple_of(start, TM_MOE), size)
            hdn = _dot(xbuf[rows, :], w1b[...]) + b1_ref[...]
            glu = jnp.minimum(hdn, SWIGLU_LIMIT)
            lin = jnp.clip(hdn, -SWIGLU_LIMIT, SWIGLU_LIMIT) + 1.0
            z = glu * jax.nn.sigmoid(SWIGLU_ALPHA * glu) * pltpu.roll(lin, 2 * TF - 1, axis=1)
            act = _dot(z.astype(BF16), sel_ref[...]).astype(BF16)
            acc[rows, :] = acc[rows, :] + _dot(act, w2b[...])

            @pl.when(j == NJ - 1)
            def _():
                outp[rows, :] = _pack_bf16_pairs(acc[rows, :])
                for g in range(size // TM_MOE):
                    out_copy(start + g * TM_MOE).start()

        def pair(r, carry):
            tile(r * (2 * TM_MOE), 2 * TM_MOE)
            return carry

        lax.fori_loop(0, nt >> 1, pair, 0)

        @pl.when((nt & 1) == 1)
        def _():
            tile((nt - 1) * TM_MOE, TM_MOE)

        @pl.when(j == NJ - 1)
        def _drain_row_stores():
            def wait(r, carry):
                out_copy(r * TM_MOE).wait()
                return carry

            lax.fori_loop(0, nt, wait, 0)

    @pl.when((c == NC_MAX - 1) & (j == NJ - 1))
    def _zero_unused_tiles():
        zero_rows = outp.at[pl.ds(0, TM_MOE), :]
        zero_rows[...] = jnp.zeros((TM_MOE, D_PACK), jnp.uint32)

        def copy(t):
            dst = out_ref.at[pl.ds(pl.multiple_of(t * TM_MOE, TM_MOE), TM_MOE), :]
            return pltpu.make_async_copy(zero_rows, dst, sem_out)

        def start(t, carry):
            copy(t).start()
            return carry

        def wait(t, carry):
            copy(t).wait()
            return carry

        lax.fori_loop(nch_ref[1], NT_SORTED, start, 0)
        lax.fori_loop(nch_ref[1], NT_SORTED, wait, 0)


def _moe_experts(layer, xs, tables, w1, b1, w2, b2):
    ce, cs, cn, nch = tables
    sel = np.zeros((2 * TF, TF), np.float32)
    sel[2 * np.arange(TF), np.arange(TF)] = 1.0

    def jeff(c, j, nch_ref):
        return jnp.where(c < nch_ref[0], j, NJ - 1)

    grid_spec = pltpu.PrefetchScalarGridSpec(
        num_scalar_prefetch=4,
        grid=(NC_MAX, NJ),
        in_specs=[
            pl.BlockSpec(memory_space=pl.ANY),
            pl.BlockSpec((None, None, D_MODEL, 2 * TF),
                         lambda c, j, ce, cs, cn, nch: (layer, ce[c], 0, jeff(c, j, nch))),
            pl.BlockSpec((None, None, 1, 2 * TF),
                         lambda c, j, ce, cs, cn, nch: (layer, ce[c], 0, jeff(c, j, nch))),
            pl.BlockSpec((None, None, TF, D_MODEL),
                         lambda c, j, ce, cs, cn, nch: (layer, ce[c], jeff(c, j, nch), 0)),
            pl.BlockSpec((None, None, 1, D_MODEL),
                         lambda c, j, ce, cs, cn, nch: (layer, ce[c], 0, 0)),
            pl.BlockSpec((2 * TF, TF), lambda c, j, ce, cs, cn, nch: (0, 0)),
        ],
        out_specs=pl.BlockSpec(memory_space=pl.ANY),
        scratch_shapes=[
            pltpu.VMEM((CH_MOE, D_MODEL), BF16),
            pltpu.VMEM((CH_MOE, D_MODEL), F32),
            pltpu.VMEM((CH_MOE, D_PACK), jnp.uint32),
            pltpu.VMEM((D_MODEL, 2 * TF), BF16),
            pltpu.VMEM((TF, D_MODEL), BF16),
            pltpu.VMEM((2, TM_MOE, D_PACK), jnp.uint32),
            pltpu.SemaphoreType.DMA((2,)),
            pltpu.SemaphoreType.DMA(()),
        ],
    )
    return pl.pallas_call(
        _moe_kernel,
        out_shape=jax.ShapeDtypeStruct((ROWS_SORTED, D_PACK), jnp.uint32),
        grid_spec=grid_spec,
        compiler_params=_cparams(2),
        name="moe_experts",
    )(ce, cs, cn, nch, xs, w1, b1.reshape(DEPTH, N_EXPERTS, 1, 2 * D_FF), w2,
      b2.reshape(DEPTH, N_EXPERTS, 1, D_MODEL), jnp.asarray(sel, BF16))


SUB_COMB = 16
PARTS_COMB = 4
CTX_TILES_COMB = T_CTX // TM_COMB


def _combine_kernel(ts_ref, code_ref, outs_ref, gw_ref, x1_ref, gf_ref, lnw_ref, lnb_ref, *rest, split):
    out_refs, (rows, sem) = rest[:-2], rest[-2:]
    part = TM_COMB // PARTS_COMB

    def start(t, carry):
        for kk in range(TOP_K):
            src = _sorted_row(ts_ref, code_ref[0, t * TOP_K + kk])
            pltpu.make_async_copy(outs_ref.at[pl.ds(src, 1), :], rows.at[kk, pl.ds(t, 1), :],
                                  sem.at[lax.shift_right_logical(t, part.bit_length() - 1)]).start(priority=kk % 2)
        return carry

    lax.fori_loop(0, TM_COMB, start, 0, unroll=2)

    def emit(dst_ref):
        for p in range(PARTS_COMB):
            def wait(t, carry, p=p):
                for kk in range(TOP_K):
                    pltpu.make_async_copy(outs_ref.at[pl.ds(0, 1), :], rows.at[kk, pl.ds(0, 1), :], sem.at[p]).wait()
                return carry

            lax.fori_loop(0, part, wait, 0)

            def sub(s, carry, p=p):
                r = pl.ds(pl.multiple_of(p * part + s * SUB_COMB, SUB_COMB), SUB_COMB)
                gw = gw_ref[r, :]
                y_lo, y_hi = _unpack_bf16_pairs(rows[0, r, :])
                y_lo, y_hi = gw[:, 0:1] * y_lo, gw[:, 0:1] * y_hi
                for kk in range(1, TOP_K):
                    lo, hi = _unpack_bf16_pairs(rows[kk, r, :])
                    y_lo, y_hi = y_lo + gw[:, kk:kk + 1] * lo, y_hi + gw[:, kk:kk + 1] * hi
                y = jnp.concatenate([y_lo, y_hi], axis=1)
                dst_ref[r, :] = _layer_norm(ALPHA * x1_ref[r, :] + gf_ref[...] * y, lnw_ref[...], lnb_ref[...])
                return carry

            lax.fori_loop(0, part // SUB_COMB, sub, 0, unroll=2)

    if split:
        is_ctx = pl.program_id(0) < CTX_TILES_COMB
        pl.when(is_ctx)(lambda: emit(out_refs[0]))
        pl.when(jnp.logical_not(is_ctx))(lambda: emit(out_refs[1]))
    else:
        emit(out_refs[0])


def _combine(layer, outs, codes, tile_start, gw, x1, mods, ln_w, ln_b, *, split):
    nblk = T_ALL // TM_COMB
    if split:
        out_shape = (jax.ShapeDtypeStruct((T_CTX, D_MODEL), F32), jax.ShapeDtypeStruct((T_LAT, D_MODEL), F32))
        out_specs = (
            pl.BlockSpec((TM_COMB, D_MODEL), lambda i, *_: (jnp.minimum(i, CTX_TILES_COMB - 1), 0)),
            pl.BlockSpec((TM_COMB, D_MODEL), lambda i, *_: (jnp.maximum(i - CTX_TILES_COMB, 0), 0)),
        )
    else:
        out_shape = jax.ShapeDtypeStruct((T_ALL, D_MODEL), F32)
        out_specs = pl.BlockSpec((TM_COMB, D_MODEL), lambda i, *_: (i, 0))
    grid_spec = pltpu.PrefetchScalarGridSpec(
        num_scalar_prefetch=1,
        grid=(nblk,),
        in_specs=[
            pl.BlockSpec((None, 1, TM_COMB * TOP_K), lambda i, *_: (i, 0, 0), memory_space=pltpu.SMEM),
            pl.BlockSpec(memory_space=pl.ANY),
            pl.BlockSpec((TM_COMB, LANES), lambda i, *_: (i, 0)),
            pl.BlockSpec((TM_COMB, D_MODEL), lambda i, *_: (i, 0)),
            _mod_spec(layer, 5, TM_COMB),
            pl.BlockSpec((None, None, 1, D_MODEL), lambda i, *_: (layer, 1, 0, 0)),
            pl.BlockSpec((None, None, 1, D_MODEL), lambda i, *_: (layer, 1, 0, 0)),
        ],
        out_specs=out_specs,
        scratch_shapes=[pltpu.VMEM((TOP_K, TM_COMB, D_PACK), jnp.uint32), pltpu.SemaphoreType.DMA((PARTS_COMB,))],
    )
    return pl.pallas_call(
        functools.partial(_combine_kernel, split=split),
        out_shape=out_shape,
        grid_spec=grid_spec,
        compiler_params=_cparams(1),
        name="moe_combine",
    )(tile_start, codes.reshape(nblk, 1, TM_COMB * TOP_K), outs, gw, x1, mods,
      ln_w.reshape(DEPTH, 2, 1, D_MODEL), ln_b.reshape(DEPTH, 2, 1, D_MODEL))


def _routing_tables(counts_f):
    counts = counts_f[0, :N_EXPERTS].astype(jnp.int32)
    ntiles = (counts + TM_MOE - 1) // TM_MOE
    tile_end = jnp.cumsum(ntiles)
    tile_start = tile_end - ntiles
    nchunks = (ntiles + TPC - 1) // TPC
    chunk_end = jnp.cumsum(nchunks)
    total = chunk_end[-1]
    cidx = jnp.arange(NC_MAX, dtype=jnp.int32)
    last = jnp.maximum(total - 1, 0)
    ce = jnp.sum((chunk_end[None, :] <= jnp.minimum(cidx, last)[:, None]).astype(jnp.int32), axis=1)
    ce = jnp.clip(ce, 0, N_EXPERTS - 1)
    kk = cidx - (chunk_end - nchunks)[ce]
    cs = tile_start[ce] + kk * TPC
    cn = jnp.where(cidx < total, jnp.clip(ntiles[ce] - kk * TPC, 0, TPC), 0)
    cs = jnp.where(cidx < total, cs, 0)
    meta = jnp.stack([total, tile_end[-1]]).astype(jnp.int32)
    tables = (ce.astype(jnp.int32), cs.astype(jnp.int32), cn.astype(jnp.int32), meta)
    return tile_start.astype(jnp.int32), tile_end.astype(jnp.int32), tables


def kernel(x_prompt, x_sample, c, cache_na_k, cache_na_v, state_mlstm_c, state_mlstm_n, state_mlstm_m,
           c_ctx, w_mod, b_mod, w_in, b_gate, mlstm_norm_w, na_rpb, w_out, ln_w, ln_b,
           router_w, router_b, expert_w1, expert_b1, expert_w2, expert_b2):
    g0 = 4 * W_M
    w_main = jnp.concatenate([w_in[:, :, :g0], w_in[:, :, g0 + N_GATES:]], axis=-1).astype(BF16)
    wg = w_in[:, :, g0:g0 + N_GATES].reshape(DEPTH, D_MODEL, 4, H_M).transpose(0, 1, 3, 2)
    w_gate = jnp.pad(wg, ((0, 0), (0, 0), (0, 0), (0, LANES - 4))).reshape(DEPTH, D_MODEL, GATE_W).astype(BF16)
    bg = b_gate.reshape(DEPTH, 4, H_M).transpose(0, 2, 1)
    bg = jnp.pad(bg, ((0, 0), (0, 0), (0, LANES - 4))).reshape(DEPTH, 1, GATE_W)
    w_out_b = w_out.astype(BF16)
    rw = jnp.pad(router_w, ((0, 0), (0, 0), (0, LANES - N_EXPERTS)))
    rb = jnp.pad(router_b, ((0, 0), (0, LANES - N_EXPERTS)), constant_values=NEG).reshape(DEPTH, 1, LANES)

    cvec = jnp.concatenate([c_ctx[None, :], c, jnp.zeros((MOD_ROWS - 1 - DEC_BATCH, D_MODEL), F32)], axis=0)
    mods = _adaln(cvec, w_mod, b_mod).reshape(DEPTH * MOD_ROWS * 6, 1, D_MODEL)

    x = jnp.concatenate([x_prompt.reshape(T_CTX, D_MODEL), x_sample.reshape(T_LAT, D_MODEL)], axis=0)
    ks_l, vs_l, cs_l, ns_l, ms_l = [], [], [], [], []
    for l in range(DEPTH):
        proj, gates, k_ctx, v_ctx = _in_proj(l, x, mods, w_main, w_gate, bg)
        m_ctx, st_c, st_n, st_m = _mlstm(l, proj, gates, mlstm_norm_w, latent=False)
        (m_lat,) = _mlstm(l, proj, gates, mlstm_norm_w, latent=True,
                          state=(state_mlstm_c, state_mlstm_n, state_mlstm_m))
        a_ctx = _attention(l, proj, latent=False)
        a_lat = _attention(l, proj, latent=True, cache_k=cache_na_k, cache_v=cache_na_v, rpb=na_rpb)
        x1, h2p, top_i, top_g = _out_proj(l, m_ctx, m_lat, a_ctx, a_lat, w_out_b, x, mods, ln_w, ln_b, rw, rb)
        code, counts = _route(top_i)
        tile_start, tile_end, tables = _routing_tables(counts)
        codes = code[:, :TOP_K].reshape(-1)
        xs = _scatter_rows(h2p, codes, tile_start, tile_end, tables[3])
        outs = _moe_experts(l, xs, tables, expert_w1, expert_b1, expert_w2, expert_b2)
        x = _combine(l, outs, codes, tile_start, top_g, x1, mods, ln_w, ln_b, split=(l == DEPTH - 1))

        ks_l.append(k_ctx.reshape(BATCH, SEQ, H_NA, D_NA))
        vs_l.append(v_ctx.reshape(BATCH, SEQ, H_NA, D_NA))
        cs_l.append(st_c)
        ns_l.append(st_n.reshape(BATCH, 2, H_M, DK_M))
        ms_l.append(st_m[:, :, :, 0, 0])

    y_prompt = x[0].reshape(BATCH, SEQ, D_MODEL)
    y_sample = x[1].reshape(DEC_BATCH, DEC_SEQ, D_MODEL)
    return (y_prompt, y_sample, jnp.stack(ks_l, axis=1), jnp.stack(vs_l, axis=1),
            jnp.stack(cs_l, axis=1), jnp.stack(ns_l, axis=1), jnp.stack(ms_l, axis=1))
```

```python
import functools

import numpy as np
import jax
import jax.numpy as jnp
from jax import lax
from jax.experimental import pallas as pl
from jax.experimental.pallas import tpu as pltpu

F32 = jnp.float32
BF16 = jnp.bfloat16

D_MODEL = 2048
BATCH, SEQ = 16, 256
DEC_BATCH, DEC_SEQ = 8, 1024
DEPTH = 2
PAST_LEN = 512
GRID_W = 64
H_M, DK_M, DV_M = 4, 256, 256
H_NA, D_NA = 8, 128
MAX_KH, KW = 8, 16
N_EXPERTS, TOP_K, D_FF = 32, 4, 2048
SWIGLU_LIMIT, SWIGLU_ALPHA = 7.0, 1.702
ROPE_BASE = 10000.0
CHUNK = 128
N_GATES = 4 * H_M
W_M = H_M * DV_M
W_NA = H_NA * D_NA
ALPHA = (2 * DEPTH) ** 0.25
NEG = -1e30
LN_EPS = 1e-5
HN_EPS = 1e-6

T_CTX = BATCH * SEQ
T_LAT = DEC_BATCH * DEC_SEQ
T_ALL = T_CTX + T_LAT
N_MAIN = 4 * W_M + 3 * W_NA
N_ASSIGN = T_ALL * TOP_K

LANES = 128
VMEM_LIMIT = 56 * 1024 * 1024

TM_IN = 1024
TN_IN = 512
TM_OUT = 256
TM_ROUTE = 256
TM_MOE = 256
CH_MOE = 2048
TPC = CH_MOE // TM_MOE
TF = 256
NJ = D_FF // TF
ROWS_SORTED = N_ASSIGN + N_EXPERTS * TM_MOE
NT_SORTED = ROWS_SORTED // TM_MOE
NC_MAX = N_EXPERTS + -(-NT_SORTED // TPC)
TM_COMB = 128
D_PACK = D_MODEL // 2
RANK_BITS = 16
assert N_ASSIGN < (1 << RANK_BITS)


def _cparams(n_axes, vmem=VMEM_LIMIT):
    return pltpu.CompilerParams(dimension_semantics=("arbitrary",) * n_axes, vmem_limit_bytes=vmem)


def _dot(a, b):
    return jnp.dot(a, b, preferred_element_type=F32)


def _split2(x):
    hi = x.astype(BF16)
    lo = (x - hi.astype(F32)).astype(BF16)
    return hi, lo


def _dot_f32(a, b):
    a1, a2 = _split2(a)
    b1, b2 = _split2(b)
    return _dot(a1, b1) + (_dot(a1, b2) + _dot(a2, b1))


def _pack_bf16_pairs(x):
    n = x.shape[1] // 2
    bits = lax.bitcast_convert_type(x.astype(BF16).astype(F32), jnp.uint32)
    return (bits[:, :n] >> 16) | (bits[:, n:] & jnp.uint32(0xFFFF0000))


def _unpack_bf16_pairs(words):
    lo = lax.bitcast_convert_type(words << 16, F32)
    hi = lax.bitcast_convert_type(words & jnp.uint32(0xFFFF0000), F32)
    return lo, hi


def _layer_norm(z, w, b):
    mu = jnp.mean(z, axis=-1, keepdims=True)
    zc = z - mu
    var = jnp.mean(zc * zc, axis=-1, keepdims=True)
    return zc * lax.rsqrt(var + LN_EPS) * w + b


MOD_ROWS = 16
TN_MOD = 512


def _adaln_kernel(c_ref, w_ref, b_ref, o_ref):
    c = c_ref[...]
    s = c * jax.nn.sigmoid(c)
    o_ref[...] = _dot_f32(s, w_ref[...]) + b_ref[...]


def _adaln(cvec, w_mod, b_mod):
    n = 6 * D_MODEL
    return pl.pallas_call(
        _adaln_kernel,
        out_shape=jax.ShapeDtypeStruct((DEPTH, MOD_ROWS, n), F32),
        grid=(DEPTH, n // TN_MOD),
        in_specs=[
            pl.BlockSpec((MOD_ROWS, D_MODEL), lambda l, j: (0, 0)),
            pl.BlockSpec((None, D_MODEL, TN_MOD), lambda l, j: (l, 0, j)),
            pl.BlockSpec((None, 1, TN_MOD), lambda l, j: (l, 0, j)),
        ],
        out_specs=pl.BlockSpec((None, MOD_ROWS, TN_MOD), lambda l, j: (l, 0, j)),
        compiler_params=_cparams(2),
        name="adaln",
    )(cvec, w_mod, b_mod.reshape(DEPTH, 1, n))


def _mod_spec(layer, k, rows_per_tile):
    ctx_tiles = T_CTX // rows_per_tile
    tiles_per_lat = DEC_SEQ // rows_per_tile

    def index(i, *_):
        r = jnp.where(i < ctx_tiles, 0, 1 + (i - ctx_tiles) // tiles_per_lat)
        return ((layer * MOD_ROWS + r) * 6 + k, 0, 0)

    return pl.BlockSpec((None, 1, D_MODEL), index)


GATE_W = H_M * LANES


CTX_TILES_IN = T_CTX // TM_IN
SEQ_PER_TILE = TM_IN // SEQ
J_NK = (4 * W_M + W_NA) // TN_IN
J_NV = (4 * W_M + 2 * W_NA) // TN_IN
KV_TILES = W_NA // TN_IN


def _inproj_kernel(x_ref, sh_ref, sc_ref, w_ref, wg_ref, bg_ref, o_ref, g_ref, ko_ref, vo_ref, h_scr):
    i = pl.program_id(0)
    j = pl.program_id(1)

    @pl.when(j == 0)
    def _():
        h = (x_ref[...] * (1.0 + sc_ref[...]) + sh_ref[...]).astype(BF16)
        h_scr[...] = h
        g_ref[...] = _dot(h, wg_ref[...]) + bg_ref[...]

    out = _dot(h_scr[...], w_ref[...])
    o_ref[...] = out
    is_ctx = i < CTX_TILES_IN

    @pl.when(is_ctx & (j >= J_NK) & (j < J_NK + KV_TILES))
    def _():
        ko_ref[...] = out.reshape(SEQ_PER_TILE, SEQ, TN_IN)

    @pl.when(is_ctx & (j >= J_NV))
    def _():
        vo_ref[...] = out.reshape(SEQ_PER_TILE, SEQ, TN_IN)


def _kv_spec(j_first):
    def index(i, j):
        jt = jnp.clip(j - j_first, 0, KV_TILES - 1)
        ctx = i < CTX_TILES_IN
        return (jnp.where(ctx, i, CTX_TILES_IN - 1), 0, jnp.where(ctx, jt, KV_TILES - 1))

    return pl.BlockSpec((SEQ_PER_TILE, SEQ, TN_IN), index)


def _in_proj(layer, x, mods, w_main, w_gate, b_gate):
    kv_shape = jax.ShapeDtypeStruct((BATCH, SEQ, W_NA), F32)
    return pl.pallas_call(
        _inproj_kernel,
        out_shape=(jax.ShapeDtypeStruct((T_ALL, N_MAIN), F32),
                   jax.ShapeDtypeStruct((T_ALL, GATE_W), F32), kv_shape, kv_shape),
        grid=(T_ALL // TM_IN, N_MAIN // TN_IN),
        in_specs=[
            pl.BlockSpec((TM_IN, D_MODEL), lambda i, j: (i, 0)),
            _mod_spec(layer, 0, TM_IN),
            _mod_spec(layer, 1, TM_IN),
            pl.BlockSpec((None, D_MODEL, TN_IN), lambda i, j: (layer, 0, j)),
            pl.BlockSpec((None, D_MODEL, GATE_W), lambda i, j: (layer, 0, 0)),
            pl.BlockSpec((None, 1, GATE_W), lambda i, j: (layer, 0, 0)),
        ],
        out_specs=(pl.BlockSpec((TM_IN, TN_IN), lambda i, j: (i, j)),
                   pl.BlockSpec((TM_IN, GATE_W), lambda i, j: (i, 0)),
                   _kv_spec(J_NK), _kv_spec(J_NV)),
        scratch_shapes=[pltpu.VMEM((TM_IN, D_MODEL), BF16)],
        compiler_params=_cparams(2),
        name="in_proj",
    )(x, mods, mods, w_main, w_gate, b_gate)


HPB = 2


def _log_sigmoid(x):
    return jnp.minimum(x, 0.0) - jnp.log(1.0 + jnp.exp(-jnp.abs(x)))


def _mlstm_kernel(*refs, seq, use_rope, has_init, emit_state, layer):
    it = iter(refs)
    q_ref, k_ref, v_ref, mo_ref, g_ref, nw_ref = (next(it) for _ in range(6))
    if use_rope:
        cos_ref, sin_ref = next(it), next(it)
    if has_init:
        c0_ref, n0_ref, m0_ref = next(it), next(it), next(it)
    out_ref = next(it)
    if emit_state:
        co_ref, no_ref, mso_ref = next(it), next(it), next(it)
    qs, ks, hs, c_scr = next(it), next(it), next(it), next(it)

    nc = seq // CHUNK
    b_idx = pl.program_id(0)
    h_idx = pl.program_id(1)

    def prep(j, carry):
        r0 = pl.multiple_of(j * CHUNK, CHUNK)
        rows = pl.ds(r0, CHUNK)
        q = q_ref[rows, :]
        k = k_ref[rows, :]
        if use_rope:
            cos = cos_ref[rows, :]
            sin = sin_ref[rows, :]

            def rope(x):
                halves = []
                for p in range(HPB * DK_M // LANES):
                    xp = x[:, p * LANES:(p + 1) * LANES]
                    t = (p % (DK_M // LANES)) * LANES
                    halves.append(xp * cos[:, t:t + LANES] + pltpu.roll(xp, LANES // 2, axis=1) * sin[:, t:t + LANES])
                return jnp.concatenate(halves, axis=1)

            q = rope(q)
            k = rope(k)
        qs[rows, :] = (q * (DK_M ** -0.5)).astype(BF16)
        ks[rows, :] = k.astype(BF16)
        return carry

    lax.fori_loop(0, nc, prep, 0)

    row_i = lax.broadcasted_iota(jnp.int32, (CHUNK, CHUNK), 0)
    col_i = lax.broadcasted_iota(jnp.int32, (CHUNK, CHUNK), 1)

    def chunk_rows(j):
        return pl.ds(pl.multiple_of(j * CHUNK, CHUNK), CHUNK)

    def head_cols(hh):
        return slice(hh * DK_M, (hh + 1) * DK_M)

    chains = [(hh, d) for hh in range(HPB) for d in range(2)]
    init = []
    for hh, d in chains:
        if has_init:
            c_scr[d, hh] = c0_ref[d, hh]
            m0 = m0_ref[((b_idx * DEPTH + layer) * 2 + d) * H_M + h_idx * HPB + hh]
            init += [n0_ref[d, hh], jnp.full((1, 1), m0, F32)]
        else:
            c_scr[d, hh] = jnp.zeros((DK_M, DV_M), F32)
            init += [jnp.zeros((1, DK_M), F32), jnp.full((1, 1), NEG, F32)]

    def chunk_step(hh, d, j, n, m):
        mask = (col_i <= row_i) if d == 0 else (col_i >= row_i)
        tri = mask.astype(BF16)
        lane_i, lane_b = 2 * d, 2 * d + 1
        end_row = CHUNK - 1 if d == 0 else 0
        rows = chunk_rows(j)
        hcols = head_cols(hh)
        gates = g_ref[rows, hh * LANES:(hh + 1) * LANES]
        lf = _log_sigmoid(gates)
        lf1, lf2 = _split2(lf)
        lf3 = (lf - lf1.astype(F32) - lf2.astype(F32)).astype(BF16)
        bc = _dot(tri, lf1) + (_dot(tri, lf2) + _dot(tri, lf3))
        gates_t = gates.T
        bc_t = bc.T
        b_col = bc[:, lane_b:lane_b + 1]
        ig_col = gates[:, lane_i:lane_i + 1]
        b_row = bc_t[lane_b:lane_b + 1, :]
        ig_row = gates_t[lane_i:lane_i + 1, :]
        b_end = bc[end_row:end_row + 1, lane_b:lane_b + 1]

        dlog = jnp.where(mask, (b_col - b_row) + ig_row, NEG)
        g = b_col + m
        mt = jnp.maximum(g, jnp.max(dlog, axis=1, keepdims=True))
        qc = qs[rows, hcols]
        kc = ks[rows, hcols]
        vc = v_ref[rows, hcols].astype(BF16)
        s = lax.dot_general(qc, kc, (((1,), (1,)), ((), ())), preferred_element_type=F32)
        s = s * jnp.exp(dlog - mt)
        sg = jnp.exp(g - mt)
        c_old = c_scr[d, hh]
        num = _dot(s.astype(BF16), vc) + sg * _dot(qc, c_old.astype(BF16))
        den = (jnp.sum(s, axis=1, keepdims=True)
               + sg * jnp.sum(qc.astype(F32) * n, axis=1, keepdims=True))
        h = num / jnp.maximum(jnp.abs(den), jnp.exp(-mt))

        wlog = (b_end - b_col) + ig_col
        m_new = jnp.maximum(b_end + m, jnp.max(wlog, axis=0, keepdims=True))
        ws = jnp.exp(wlog - m_new)
        sc = jnp.exp(b_end + m - m_new)
        kw = kc.astype(F32) * ws
        c_scr[d, hh] = sc * c_old + _dot(kw.T.astype(BF16), vc)
        n_new = sc * n + jnp.sum(kw, axis=0, keepdims=True)
        return h, n_new, m_new

    def finish(hh, j, ht):
        rows = chunk_rows(j)
        hcols = head_cols(hh)
        mu = jnp.mean(ht, axis=1, keepdims=True)
        hc = ht - mu
        var = jnp.mean(hc * hc, axis=1, keepdims=True)
        hn = hc * lax.rsqrt(var + HN_EPS) * nw_ref[:, hcols]
        out_ref[rows, hcols] = (jax.nn.sigmoid(mo_ref[rows, hcols]) * hn).astype(out_ref.dtype)

    def advance(jj, carry, complete):
        new = []
        for idx, (hh, d) in enumerate(chains):
            j = jj if d == 0 else nc - 1 - jj
            h, n, m = chunk_step(hh, d, j, carry[2 * idx], carry[2 * idx + 1])
            new += [n, m]
            if complete:
                finish(hh, j, hs[chunk_rows(j), head_cols(hh)] + h)
            else:
                hs[chunk_rows(j), head_cols(hh)] = h
        return tuple(new)

    carry = lax.fori_loop(0, nc // 2, functools.partial(advance, complete=False), tuple(init))
    carry = lax.fori_loop(nc // 2, nc, functools.partial(advance, complete=True), carry)
    if emit_state:
        for idx, (hh, d) in enumerate(chains):
            co_ref[d, hh] = c_scr[d, hh]
            no_ref[d, hh] = carry[2 * idx]
            mso_ref[d, hh] = jnp.broadcast_to(carry[2 * idx + 1], (1, LANES))


def _rope_tables(seq):
    pos = np.arange(seq)
    half = DK_M // 2
    nf = half // 2
    inv = ROPE_BASE ** (-np.arange(nf, dtype=np.float32) / nf)
    cos_parts, sin_parts = [], []
    for p in ((pos // GRID_W).astype(np.float32), (pos % GRID_W).astype(np.float32)):
        ang = (p[:, None] * inv).astype(np.float32)
        cos_parts += [np.cos(ang), np.cos(ang)]
        sin_parts += [-np.sin(ang), np.sin(ang)]
    return (jnp.asarray(np.concatenate(cos_parts, axis=1), F32),
            jnp.asarray(np.concatenate(sin_parts, axis=1), F32))


def _mlstm(layer, proj, gates, norm_w, *, latent, state=None):
    if latent:
        nb, seq, rb0 = DEC_BATCH, DEC_SEQ, T_CTX // DEC_SEQ
    else:
        nb, seq, rb0 = BATCH, SEQ, 0
    cb = W_M // (HPB * DK_M)
    wq = HPB * DK_M
    in_specs = [
        pl.BlockSpec((seq, wq), lambda b, h: (rb0 + b, h)),
        pl.BlockSpec((seq, wq), lambda b, h: (rb0 + b, cb + h)),
        pl.BlockSpec((seq, wq), lambda b, h: (rb0 + b, 2 * cb + h)),
        pl.BlockSpec((seq, wq), lambda b, h: (rb0 + b, 3 * cb + h)),
        pl.BlockSpec((seq, HPB * LANES), lambda b, h: (rb0 + b, h)),
        pl.BlockSpec((None, 1, wq), lambda b, h: (layer, 0, h)),
    ]
    args = [proj, proj, proj, proj, gates, norm_w.reshape(DEPTH, 1, W_M)]
    if latent:
        cos, sin = _rope_tables(seq)
        in_specs += [pl.BlockSpec((seq, DK_M), lambda b, h: (0, 0))] * 2
        args += [cos, sin]
        st_c, st_n, st_m = state
        in_specs += [
            pl.BlockSpec((None, None, 2, HPB, DK_M, DV_M), lambda b, h: (b, layer, 0, h, 0, 0)),
            pl.BlockSpec((None, None, 2, HPB, 1, DK_M), lambda b, h: (b, layer, 0, h, 0, 0)),
            pl.BlockSpec(memory_space=pltpu.SMEM),
        ]
        args += [st_c, st_n.reshape(DEC_BATCH, DEPTH, 2, H_M, 1, DK_M), st_m.reshape(-1)]
    out_shape = [jax.ShapeDtypeStruct((nb * seq, W_M), BF16)]
    out_specs = [pl.BlockSpec((seq, wq), lambda b, h: (b, h))]
    if not latent:
        out_shape += [
            jax.ShapeDtypeStruct((nb, 2, H_M, DK_M, DV_M), F32),
            jax.ShapeDtypeStruct((nb, 2, H_M, 1, DK_M), F32),
            jax.ShapeDtypeStruct((nb, 2, H_M, 1, LANES), F32),
        ]
        out_specs += [
            pl.BlockSpec((None, 2, HPB, DK_M, DV_M), lambda b, h: (b, 0, h, 0, 0)),
            pl.BlockSpec((None, 2, HPB, 1, DK_M), lambda b, h: (b, 0, h, 0, 0)),
            pl.BlockSpec((None, 2, HPB, 1, LANES), lambda b, h: (b, 0, h, 0, 0)),
        ]
    kern = functools.partial(_mlstm_kernel, seq=seq, use_rope=latent, has_init=latent,
                             emit_state=not latent, layer=layer)
    return pl.pallas_call(
        kern,
        out_shape=tuple(out_shape),
        grid=(nb, H_M // HPB),
        in_specs=in_specs,
        out_specs=tuple(out_specs),
        scratch_shapes=[
            pltpu.VMEM((seq, wq), BF16),
            pltpu.VMEM((seq, wq), BF16),
            pltpu.VMEM((seq, wq), F32),
            pltpu.VMEM((2, HPB, DK_M, DV_M), F32),
        ],
        compiler_params=_cparams(2),
        name="mlstm_lat" if latent else "mlstm_ctx",
    )(*args)


RPB_ROWS = 2 * MAX_KH - 1
RPB_COLS = 2 * KW - 1
Q_TILE = 256
LAT_ROWS = DEC_SEQ // GRID_W
KH = min(MAX_KH, LAT_ROWS)


def _attn_ctx_kernel(q_ref, k_ref, v_ref, o_ref):
    scale = D_NA ** -0.5
    for h in range(H_NA):
        cols = slice(h * D_NA, (h + 1) * D_NA)
        q = q_ref[:, cols].astype(BF16)
        k = k_ref[:, cols].astype(BF16)
        v = v_ref[:, cols].astype(BF16)
        s = lax.dot_general(q, k, (((1,), (1,)), ((), ())), preferred_element_type=F32) * scale
        m = jnp.max(s, axis=1, keepdims=True)
        p = jnp.exp(s - m)
        l = jnp.sum(p, axis=1, keepdims=True)
        o_ref[:, cols] = (_dot(p.astype(BF16), v) / l).astype(o_ref.dtype)


def _attn_lat_kernel(rpb_ref, q_ref, k_ref, v_ref, kc_ref, vc_ref, o_ref, bias_scr, g_scr):
    h_idx = pl.program_id(0)
    scale = D_NA ** -0.5

    @pl.when(pl.program_id(1) == 0)
    def _build_bias():
        qc = lax.broadcasted_iota(jnp.int32, (GRID_W, LANES), 0)
        lane = lax.broadcasted_iota(jnp.int32, (GRID_W, LANES), 1)
        kc = lane & (GRID_W - 1)
        upper = lane >= GRID_W
        cidx = jnp.clip(kc - qc, -(KW - 1), KW - 1) + (KW - 1)
        cs = jnp.clip(qc - KW // 2, 0, GRID_W - KW)
        col_ok = (kc >= cs) & (kc < cs + KW)
        base = h_idx * (RPB_ROWS * RPB_COLS)
        for dr0 in range(-KH, KH):
            acc = jnp.zeros((GRID_W, LANES), F32)
            for c in range(RPB_COLS):
                lo = rpb_ref[base + (dr0 + MAX_KH - 1) * RPB_COLS + c] if -MAX_KH < dr0 < MAX_KH else 0.0
                hi = rpb_ref[base + (dr0 + MAX_KH) * RPB_COLS + c] if -MAX_KH < dr0 + 1 < MAX_KH else 0.0
                acc = jnp.where(cidx == c, jnp.where(upper, hi, lo), acc)
            g_scr[dr0 + KH] = jnp.where(col_ok, acc, NEG)
        neg_tile = jnp.full((GRID_W, LANES), NEG, F32)
        for qr in range(LAT_ROWS):
            rs = min(max(qr - KH // 2, 0), LAT_ROWS - KH)
            for t in range(LAT_ROWS // 2):
                ok0 = rs <= 2 * t < rs + KH
                ok1 = rs <= 2 * t + 1 < rs + KH
                if ok0 or ok1:
                    tile = g_scr[2 * t - qr + KH]
                    if not ok1:
                        tile = jnp.where(upper, NEG, tile)
                    if not ok0:
                        tile = jnp.where(upper, tile, NEG)
                else:
                    tile = neg_tile
                bias_scr[qr * GRID_W:(qr + 1) * GRID_W, t * LANES:(t + 1) * LANES] = tile

    k = k_ref[...].astype(BF16)
    v = v_ref[...].astype(BF16)
    kc = kc_ref[...].astype(BF16)
    vc = vc_ref[...].astype(BF16)
    dn = (((1,), (1,)), ((), ()))
    for qt in range(DEC_SEQ // Q_TILE):
        rows = slice(qt * Q_TILE, (qt + 1) * Q_TILE)
        q = q_ref[rows, :].astype(BF16)
        s_loc = lax.dot_general(q, k, dn, preferred_element_type=F32) * scale + bias_scr[rows, :]
        s_ctx = lax.dot_general(q, kc, dn, preferred_element_type=F32) * scale
        m = jnp.maximum(jnp.max(s_loc, axis=1, keepdims=True), jnp.max(s_ctx, axis=1, keepdims=True))
        p_loc = jnp.exp(s_loc - m)
        p_ctx = jnp.exp(s_ctx - m)
        l = jnp.sum(p_loc, axis=1, keepdims=True) + jnp.sum(p_ctx, axis=1, keepdims=True)
        o = _dot(p_loc.astype(BF16), v) + _dot(p_ctx.astype(BF16), vc)
        o_ref[rows, :] = (o / l).astype(o_ref.dtype)


def _attention(layer, proj, *, latent, cache_k=None, cache_v=None, rpb=None):
    qb0 = 4 * W_M // D_NA
    if not latent:
        return pl.pallas_call(
            _attn_ctx_kernel,
            out_shape=jax.ShapeDtypeStruct((T_CTX, W_NA), BF16),
            grid=(BATCH,),
            in_specs=[
                pl.BlockSpec((SEQ, W_NA), lambda b: (b, qb0 // H_NA)),
                pl.BlockSpec((SEQ, W_NA), lambda b: (b, qb0 // H_NA + 1)),
                pl.BlockSpec((SEQ, W_NA), lambda b: (b, qb0 // H_NA + 2)),
            ],
            out_specs=pl.BlockSpec((SEQ, W_NA), lambda b: (b, 0)),
            compiler_params=_cparams(1),
            name="attn_ctx",
        )(proj, proj, proj)
    rb0 = T_CTX // DEC_SEQ
    ck = cache_k.reshape(DEC_BATCH, DEPTH, PAST_LEN, W_NA)
    cv = cache_v.reshape(DEC_BATCH, DEPTH, PAST_LEN, W_NA)
    grid_spec = pltpu.PrefetchScalarGridSpec(
        num_scalar_prefetch=1,
        grid=(H_NA, DEC_BATCH),
        in_specs=[
            pl.BlockSpec((DEC_SEQ, D_NA), lambda h, b, r: (rb0 + b, qb0 + h)),
            pl.BlockSpec((DEC_SEQ, D_NA), lambda h, b, r: (rb0 + b, qb0 + H_NA + h)),
            pl.BlockSpec((DEC_SEQ, D_NA), lambda h, b, r: (rb0 + b, qb0 + 2 * H_NA + h)),
            pl.BlockSpec((None, None, PAST_LEN, D_NA), lambda h, b, r: (b, layer, 0, h)),
            pl.BlockSpec((None, None, PAST_LEN, D_NA), lambda h, b, r: (b, layer, 0, h)),
        ],
        out_specs=pl.BlockSpec((DEC_SEQ, D_NA), lambda h, b, r: (b, h)),
        scratch_shapes=[
            pltpu.VMEM((DEC_SEQ, DEC_SEQ), F32),
            pltpu.VMEM((2 * KH, GRID_W, LANES), F32),
        ],
    )
    return pl.pallas_call(
        _attn_lat_kernel,
        out_shape=jax.ShapeDtypeStruct((T_LAT, W_NA), BF16),
        grid_spec=grid_spec,
        compiler_params=_cparams(2),
        name="attn_lat",
    )(rpb[layer].reshape(-1), proj, proj, proj, ck, cv)


def _outproj_kernel(mc_ref, ml_ref, ac_ref, al_ref, w_ref, x_ref, ga_ref, shf_ref, scf_ref,
                    lnw_ref, lnb_ref, rw_ref, rb_ref, x1_ref, h2_ref, ti_ref, tg_ref):
    is_ctx = pl.program_id(0) < T_CTX // TM_OUT
    m = jnp.where(is_ctx, mc_ref[...], ml_ref[...])
    a = jnp.where(is_ctx, ac_ref[...], al_ref[...])
    y = _dot(m, w_ref[0:W_M, :]) + _dot(a, w_ref[W_M:W_M + W_NA, :])
    x1 = _layer_norm(ALPHA * x_ref[...] + ga_ref[...] * y, lnw_ref[...], lnb_ref[...])
    x1_ref[...] = x1
    h2 = x1 * (1.0 + scf_ref[...]) + shf_ref[...]
    h2_ref[...] = _pack_bf16_pairs(h2)
    logits = _dot_f32(h2, rw_ref[...]) + rb_ref[...]
    lane = lax.broadcasted_iota(jnp.int32, logits.shape, 1)
    vals = logits
    top_v, top_i = [], []
    for _ in range(TOP_K):
        mx = jnp.max(vals, axis=1, keepdims=True)
        ix = jnp.min(jnp.where(vals == mx, lane, LANES), axis=1, keepdims=True)
        top_v.append(mx)
        top_i.append(ix)
        vals = jnp.where(lane == ix, -jnp.inf, vals)
    ex = [jnp.exp(v - top_v[0]) for v in top_v]
    den = ex[0] + ex[1] + ex[2] + ex[3]
    ti = jnp.zeros(logits.shape, jnp.int32)
    tg = jnp.zeros(logits.shape, F32)
    for kk in range(TOP_K):
        ti = jnp.where(lane == kk, top_i[kk], ti)
        tg = jnp.where(lane == kk, ex[kk] / den, tg)
    ti_ref[...] = ti
    tg_ref[...] = tg


def _out_proj(layer, m_ctx, m_lat, a_ctx, a_lat, w_out, x, mods, ln_w, ln_b, router_w, router_b):
    nctx = T_CTX // TM_OUT
    ctx_spec = lambda w: pl.BlockSpec((TM_OUT, w), lambda i: (jnp.minimum(i, nctx - 1), 0))
    lat_spec = lambda w: pl.BlockSpec((TM_OUT, w), lambda i: (jnp.maximum(i - nctx, 0), 0))
    row_spec = lambda w: pl.BlockSpec((TM_OUT, w), lambda i: (i, 0))
    return pl.pallas_call(
        _outproj_kernel,
        out_shape=(jax.ShapeDtypeStruct((T_ALL, D_MODEL), F32),
                   jax.ShapeDtypeStruct((T_ALL, D_PACK), jnp.uint32),
                   jax.ShapeDtypeStruct((T_ALL, LANES), jnp.int32),
                   jax.ShapeDtypeStruct((T_ALL, LANES), F32)),
        grid=(T_ALL // TM_OUT,),
        in_specs=[
            ctx_spec(W_M), lat_spec(W_M), ctx_spec(W_NA), lat_spec(W_NA),
            pl.BlockSpec((None, W_M + W_NA, D_MODEL), lambda i: (layer, 0, 0)),
            row_spec(D_MODEL),
            _mod_spec(layer, 2, TM_OUT), _mod_spec(layer, 3, TM_OUT), _mod_spec(layer, 4, TM_OUT),
            pl.BlockSpec((None, None, 1, D_MODEL), lambda i: (layer, 0, 0, 0)),
            pl.BlockSpec((None, None, 1, D_MODEL), lambda i: (layer, 0, 0, 0)),
            pl.BlockSpec((None, D_MODEL, LANES), lambda i: (layer, 0, 0)),
            pl.BlockSpec((None, 1, LANES), lambda i: (layer, 0, 0)),
        ],
        out_specs=(row_spec(D_MODEL), row_spec(D_PACK), row_spec(LANES), row_spec(LANES)),
        compiler_params=_cparams(1),
        name="out_proj",
    )(m_ctx, m_lat, a_ctx, a_lat, w_out, x, mods, mods, mods,
      ln_w.reshape(DEPTH, 2, 1, D_MODEL), ln_b.reshape(DEPTH, 2, 1, D_MODEL), router_w, router_b)


def _route_kernel(ti_ref, code_ref, cnt_ref, carry):
    @pl.when(pl.program_id(0) == 0)
    def _():
        carry[...] = jnp.zeros((1, LANES), F32)

    ti = ti_ref[...]
    lane = lax.broadcasted_iota(jnp.int32, (TM_ROUTE, LANES), 1)
    hits = [lane == ti[:, kk:kk + 1] for kk in range(TOP_K)]
    cnt = hits[0].astype(F32)
    for kk in range(1, TOP_K):
        cnt = cnt + hits[kk].astype(F32)
    row_i = lax.broadcasted_iota(jnp.int32, (TM_ROUTE, TM_ROUTE), 0)
    col_i = lax.broadcasted_iota(jnp.int32, (TM_ROUTE, TM_ROUTE), 1)
    before = _dot((col_i < row_i).astype(BF16), cnt.astype(BF16)) + carry[...]
    code = jnp.zeros((TM_ROUTE, LANES), jnp.int32)
    for kk in range(TOP_K):
        rank = jnp.sum(jnp.where(hits[kk], before, 0.0), axis=1, keepdims=True).astype(jnp.int32)
        code = jnp.where(lane == kk, (ti[:, kk:kk + 1] << RANK_BITS) | rank, code)
    code_ref[...] = code
    carry[...] = carry[...] + jnp.sum(cnt, axis=0, keepdims=True)
    cnt_ref[...] = carry[...]


def _route(top_i):
    return pl.pallas_call(
        _route_kernel,
        out_shape=(jax.ShapeDtypeStruct((T_ALL, LANES), jnp.int32),
                   jax.ShapeDtypeStruct((1, LANES), F32)),
        grid=(T_ALL // TM_ROUTE,),
        in_specs=[pl.BlockSpec((TM_ROUTE, LANES), lambda i: (i, 0))],
        out_specs=(pl.BlockSpec((TM_ROUTE, LANES), lambda i: (i, 0)),
                   pl.BlockSpec((1, LANES), lambda i: (0, 0))),
        scratch_shapes=[pltpu.VMEM((1, LANES), F32)],
        compiler_params=_cparams(1),
        name="moe_route",
    )(top_i)


def _sorted_rows(code, tile_start):
    expert = code[:, :TOP_K] >> RANK_BITS
    rank = code[:, :TOP_K] & ((1 << RANK_BITS) - 1)
    first = jnp.zeros_like(expert)
    for e in range(N_EXPERTS):
        first = jnp.where(expert == e, tile_start[e] * TM_MOE, first)
    return (first + rank).reshape(-1)


def _scatter_kernel(ts_ref, te_ref, meta_ref, row_ref, src_ref, xs_ref, zero_scr, sem):
    @pl.when(pl.program_id(0) == 0)
    def _define_padding():
        zero_scr[...] = jnp.zeros((TM_MOE, D_PACK), jnp.uint32)

        def fill(t):
            dst = xs_ref.at[pl.ds(pl.multiple_of(t * TM_MOE, TM_MOE), TM_MOE), :]
            return pltpu.make_async_copy(zero_scr, dst, sem)

        def per_expert(fn):
            def body(e, carry):
                @pl.when(te_ref[e] > ts_ref[e])
                def _():
                    fn(fill(te_ref[e] - 1))
                return carry
            lax.fori_loop(0, N_EXPERTS, body, 0)

        def tail(fn):
            def body(t, carry):
                fn(fill(t))
                return carry
            lax.fori_loop(meta_ref[1], NT_SORTED, body, 0)

        per_expert(lambda cp: cp.start())
        tail(lambda cp: cp.start())
        per_expert(lambda cp: cp.wait())
        tail(lambda cp: cp.wait())

    def start(t, carry):
        for kk in range(TOP_K):
            dst = row_ref[0, t * TOP_K + kk]
            pltpu.make_async_copy(src_ref.at[pl.ds(t, 1), :], xs_ref.at[pl.ds(dst, 1), :], sem).start(priority=kk % 2)
        return carry

    def wait(t, carry):
        for kk in range(TOP_K):
            pltpu.make_async_copy(src_ref.at[pl.ds(0, 1), :], xs_ref.at[pl.ds(0, 1), :], sem).wait()
        return carry

    lax.fori_loop(0, TM_ROUTE, start, 0, unroll=True)
    lax.fori_loop(0, TM_ROUTE, wait, 0)


def _scatter_rows(src, sorted_rows, tile_start, tile_end, meta):
    nblk = T_ALL // TM_ROUTE
    grid_spec = pltpu.PrefetchScalarGridSpec(
        num_scalar_prefetch=3,
        grid=(nblk,),
        in_specs=[
            pl.BlockSpec((None, 1, TM_ROUTE * TOP_K), lambda i, *_: (i, 0, 0), memory_space=pltpu.SMEM),
            pl.BlockSpec((TM_ROUTE, D_PACK), lambda i, *_: (i, 0)),
        ],
        out_specs=pl.BlockSpec(memory_space=pl.ANY),
        scratch_shapes=[pltpu.VMEM((TM_MOE, D_PACK), jnp.uint32), pltpu.SemaphoreType.DMA(())],
    )
    return pl.pallas_call(
        _scatter_kernel,
        out_shape=jax.ShapeDtypeStruct((ROWS_SORTED, D_PACK), jnp.uint32),
        grid_spec=grid_spec,
        compiler_params=_cparams(1),
        name="moe_scatter",
    )(tile_start, tile_end, meta, sorted_rows.reshape(nblk, 1, TM_ROUTE * TOP_K), src)


def _moe_kernel(ce_ref, cs_ref, cn_ref, nch_ref, xs_ref, w1_ref, b1_ref, w2_ref, b2_ref, sel_ref,
                out_ref, xbuf, acc, outp, stage, sem_in, sem_out):
    c = pl.program_id(0)
    j = pl.program_id(1)
    nt = cn_ref[c]
    row0 = cs_ref[c] * TM_MOE

    def tile_rows(r):
        return pl.ds(pl.multiple_of(r * TM_MOE, TM_MOE), TM_MOE)

    @pl.when(nt > 0)
    def _chunk():
        @pl.when(j == 0)
        def _load_rows():
            def fetch(r, slot):
                src = xs_ref.at[pl.ds(pl.multiple_of(row0 + r * TM_MOE, TM_MOE), TM_MOE), :]
                return pltpu.make_async_copy(src, stage.at[slot], sem_in.at[slot])

            fetch(0, 0).start()

            def load(r, carry):
                slot = r & 1
                fetch(r, slot).wait()

                @pl.when(r + 1 < nt)
                def _():
                    fetch(r + 1, 1 - slot).start()

                lo, hi = _unpack_bf16_pairs(stage[slot])
                xbuf[tile_rows(r), 0:D_PACK] = lo.astype(BF16)
                xbuf[tile_rows(r), D_PACK:D_MODEL] = hi.astype(BF16)
                acc[tile_rows(r), :] = jnp.broadcast_to(b2_ref[...], (TM_MOE, D_MODEL))
                return carry

            lax.fori_loop(0, nt, load, 0)

        def out_copy(start):
            rows = pl.ds(pl.multiple_of(start, TM_MOE), TM_MOE)
            dst = out_ref.at[pl.ds(pl.multiple_of(row0 + start, TM_MOE), TM_MOE), :]
            return pltpu.make_async_copy(outp.at[rows, :], dst, sem_out)

        def tile(start, size):
            rows = pl.ds(pl.multiple_of(start, TM_MOE), size)
            hdn = _dot(xbuf[rows, :], w1_ref[...].astype(BF16)) + b1_ref[...]
            glu = jnp.minimum(hdn, SWIGLU_LIMIT)
            lin = jnp.clip(hdn, -SWIGLU_LIMIT, SWIGLU_LIMIT) + 1.0
            z = glu * jax.nn.sigmoid(SWIGLU_ALPHA * glu) * pltpu.roll(lin, 2 * TF - 1, axis=1)
            act = _dot(z.astype(BF16), sel_ref[...]).astype(BF16)
            acc[rows, :] = acc[rows, :] + _dot(act, w2_ref[...].astype(BF16))

            @pl.when(j == NJ - 1)
            def _():
                outp[rows, :] = _pack_bf16_pairs(acc[rows, :])
                for g in range(size // TM_MOE):
                    out_copy(start + g * TM_MOE).start()

        def pair(r, carry):
            tile(r * (2 * TM_MOE), 2 * TM_MOE)
            return carry

        lax.fori_loop(0, nt >> 1, pair, 0)

        @pl.when((nt & 1) == 1)
        def _():
            tile((nt - 1) * TM_MOE, TM_MOE)

        @pl.when(j == NJ - 1)
        def _drain_row_stores():
            def wait(r, carry):
                out_copy(r * TM_MOE).wait()
                return carry

            lax.fori_loop(0, nt, wait, 0)

    @pl.when((c == NC_MAX - 1) & (j == NJ - 1))
    def _zero_unused_tiles():
        zero_rows = outp.at[pl.ds(0, TM_MOE), :]
        zero_rows[...] = jnp.zeros((TM_MOE, D_PACK), jnp.uint32)

        def copy(t):
            dst = out_ref.at[pl.ds(pl.multiple_of(t * TM_MOE, TM_MOE), TM_MOE), :]
            return pltpu.make_async_copy(zero_rows, dst, sem_out)

        def start(t, carry):
            copy(t).start()
            return carry

        def wait(t, carry):
            copy(t).wait()
            return carry

        lax.fori_loop(nch_ref[1], NT_SORTED, start, 0)
        lax.fori_loop(nch_ref[1], NT_SORTED, wait, 0)


def _moe_experts(layer, xs, tables, w1, b1, w2, b2):
    ce, cs, cn, nch = tables
    sel = np.zeros((2 * TF, TF), np.float32)
    sel[2 * np.arange(TF), np.arange(TF)] = 1.0

    def jeff(c, j, nch_ref):
        return jnp.where(c < nch_ref[0], j, NJ - 1)

    grid_spec = pltpu.PrefetchScalarGridSpec(
        num_scalar_prefetch=4,
        grid=(NC_MAX, NJ),
        in_specs=[
            pl.BlockSpec(memory_space=pl.ANY),
            pl.BlockSpec((None, None, D_MODEL, 2 * TF),
                         lambda c, j, ce, cs, cn, nch: (layer, ce[c], 0, jeff(c, j, nch))),
            pl.BlockSpec((None, None, 1, 2 * TF),
                         lambda c, j, ce, cs, cn, nch: (layer, ce[c], 0, jeff(c, j, nch))),
            pl.BlockSpec((None, None, TF, D_MODEL),
                         lambda c, j, ce, cs, cn, nch: (layer, ce[c], jeff(c, j, nch), 0)),
            pl.BlockSpec((None, None, 1, D_MODEL),
                         lambda c, j, ce, cs, cn, nch: (layer, ce[c], 0, 0)),
            pl.BlockSpec((2 * TF, TF), lambda c, j, ce, cs, cn, nch: (0, 0)),
        ],
        out_specs=pl.BlockSpec(memory_space=pl.ANY),
        scratch_shapes=[
            pltpu.VMEM((CH_MOE, D_MODEL), BF16),
            pltpu.VMEM((CH_MOE, D_MODEL), F32),
            pltpu.VMEM((CH_MOE, D_PACK), jnp.uint32),
            pltpu.VMEM((2, TM_MOE, D_PACK), jnp.uint32),
            pltpu.SemaphoreType.DMA((2,)),
            pltpu.SemaphoreType.DMA(()),
        ],
    )
    return pl.pallas_call(
        _moe_kernel,
        out_shape=jax.ShapeDtypeStruct((ROWS_SORTED, D_PACK), jnp.uint32),
        grid_spec=grid_spec,
        compiler_params=_cparams(2),
        name="moe_experts",
    )(ce, cs, cn, nch, xs, w1, b1.reshape(DEPTH, N_EXPERTS, 1, 2 * D_FF), w2,
      b2.reshape(DEPTH, N_EXPERTS, 1, D_MODEL), jnp.asarray(sel, BF16))


SUB_COMB = 16
PARTS_COMB = 4
CTX_TILES_COMB = T_CTX // TM_COMB


def _combine_kernel(row_ref, outs_ref, gw_ref, x1_ref, gf_ref, lnw_ref, lnb_ref, *rest, split):
    out_refs, (rows, sem) = rest[:-2], rest[-2:]
    part = TM_COMB // PARTS_COMB

    def start(t, carry):
        for kk in range(TOP_K):
            src = row_ref[0, t * TOP_K + kk]
            pltpu.make_async_copy(outs_ref.at[pl.ds(src, 1), :], rows.at[kk, pl.ds(t, 1), :],
                                  sem.at[lax.shift_right_logical(t, part.bit_length() - 1)]).start(priority=kk % 2)
        return carry

    lax.fori_loop(0, TM_COMB, start, 0, unroll=True)

    def emit(dst_ref):
        for p in range(PARTS_COMB):
            def wait(t, carry, p=p):
                for kk in range(TOP_K):
                    pltpu.make_async_copy(outs_ref.at[pl.ds(0, 1), :], rows.at[kk, pl.ds(0, 1), :], sem.at[p]).wait()
                return carry

            lax.fori_loop(0, part, wait, 0)

            def sub(s, carry, p=p):
                r = pl.ds(pl.multiple_of(p * part + s * SUB_COMB, SUB_COMB), SUB_COMB)
                gw = gw_ref[r, :]
                y_lo, y_hi = _unpack_bf16_pairs(rows[0, r, :])
                y_lo, y_hi = gw[:, 0:1] * y_lo, gw[:, 0:1] * y_hi
                for kk in range(1, TOP_K):
                    lo, hi = _unpack_bf16_pairs(rows[kk, r, :])
                    y_lo, y_hi = y_lo + gw[:, kk:kk + 1] * lo, y_hi + gw[:, kk:kk + 1] * hi
                y = jnp.concatenate([y_lo, y_hi], axis=1)
                dst_ref[r, :] = _layer_norm(ALPHA * x1_ref[r, :] + gf_ref[...] * y, lnw_ref[...], lnb_ref[...])
                return carry

            lax.fori_loop(0, part // SUB_COMB, sub, 0, unroll=2)

    if split:
        is_ctx = pl.program_id(0) < CTX_TILES_COMB
        pl.when(is_ctx)(lambda: emit(out_refs[0]))
        pl.when(jnp.logical_not(is_ctx))(lambda: emit(out_refs[1]))
    else:
        emit(out_refs[0])


def _combine(layer, outs, sorted_rows, gw, x1, mods, ln_w, ln_b, *, split):
    nblk = T_ALL // TM_COMB
    if split:
        out_shape = (jax.ShapeDtypeStruct((T_CTX, D_MODEL), F32), jax.ShapeDtypeStruct((T_LAT, D_MODEL), F32))
        out_specs = (
            pl.BlockSpec((TM_COMB, D_MODEL), lambda i, *_: (jnp.minimum(i, CTX_TILES_COMB - 1), 0)),
            pl.BlockSpec((TM_COMB, D_MODEL), lambda i, *_: (jnp.maximum(i - CTX_TILES_COMB, 0), 0)),
        )
    else:
        out_shape = jax.ShapeDtypeStruct((T_ALL, D_MODEL), F32)
        out_specs = pl.BlockSpec((TM_COMB, D_MODEL), lambda i, *_: (i, 0))
    grid_spec = pltpu.PrefetchScalarGridSpec(
        num_scalar_prefetch=0,
        grid=(nblk,),
        in_specs=[
            pl.BlockSpec((None, 1, TM_COMB * TOP_K), lambda i, *_: (i, 0, 0), memory_space=pltpu.SMEM),
            pl.BlockSpec(memory_space=pl.ANY),
            pl.BlockSpec((TM_COMB, LANES), lambda i, *_: (i, 0)),
            pl.BlockSpec((TM_COMB, D_MODEL), lambda i, *_: (i, 0)),
            _mod_spec(layer, 5, TM_COMB),
            pl.BlockSpec((None, None, 1, D_MODEL), lambda i, *_: (layer, 1, 0, 0)),
            pl.BlockSpec((None, None, 1, D_MODEL), lambda i, *_: (layer, 1, 0, 0)),
        ],
        out_specs=out_specs,
        scratch_shapes=[pltpu.VMEM((TOP_K, TM_COMB, D_PACK), jnp.uint32), pltpu.SemaphoreType.DMA((PARTS_COMB,))],
    )
    return pl.pallas_call(
        functools.partial(_combine_kernel, split=split),
        out_shape=out_shape,
        grid_spec=grid_spec,
        compiler_params=_cparams(1),
        name="moe_combine",
    )(sorted_rows.reshape(nblk, 1, TM_COMB * TOP_K), outs, gw, x1, mods,
      ln_w.reshape(DEPTH, 2, 1, D_MODEL), ln_b.reshape(DEPTH, 2, 1, D_MODEL))


def _routing_tables(counts_f):
    counts = counts_f[0, :N_EXPERTS].astype(jnp.int32)
    ntiles = (counts + TM_MOE - 1) // TM_MOE
    tile_end = jnp.cumsum(ntiles)
    tile_start = tile_end - ntiles
    nchunks = (ntiles + TPC - 1) // TPC
    chunk_end = jnp.cumsum(nchunks)
    total = chunk_end[-1]
    cidx = jnp.arange(NC_MAX, dtype=jnp.int32)
    last = jnp.maximum(total - 1, 0)
    ce = jnp.sum((chunk_end[None, :] <= jnp.minimum(cidx, last)[:, None]).astype(jnp.int32), axis=1)
    ce = jnp.clip(ce, 0, N_EXPERTS - 1)
    kk = cidx - (chunk_end - nchunks)[ce]
    cs = tile_start[ce] + kk * TPC
    cn = jnp.where(cidx < total, jnp.clip(ntiles[ce] - kk * TPC, 0, TPC), 0)
    cs = jnp.where(cidx < total, cs, 0)
    meta = jnp.stack([total, tile_end[-1]]).astype(jnp.int32)
    tables = (ce.astype(jnp.int32), cs.astype(jnp.int32), cn.astype(jnp.int32), meta)
    return tile_start.astype(jnp.int32), tile_end.astype(jnp.int32), tables


def kernel(x_prompt, x_sample, c, cache_na_k, cache_na_v, state_mlstm_c, state_mlstm_n, state_mlstm_m,
           c_ctx, w_mod, b_mod, w_in, b_gate, mlstm_norm_w, na_rpb, w_out, ln_w, ln_b,
           router_w, router_b, expert_w1, expert_b1, expert_w2, expert_b2):
    g0 = 4 * W_M
    w_main = jnp.concatenate([w_in[:, :, :g0], w_in[:, :, g0 + N_GATES:]], axis=-1).astype(BF16)
    wg = w_in[:, :, g0:g0 + N_GATES].reshape(DEPTH, D_MODEL, 4, H_M).transpose(0, 1, 3, 2)
    w_gate = jnp.pad(wg, ((0, 0), (0, 0), (0, 0), (0, LANES - 4))).reshape(DEPTH, D_MODEL, GATE_W).astype(BF16)
    bg = b_gate.reshape(DEPTH, 4, H_M).transpose(0, 2, 1)
    bg = jnp.pad(bg, ((0, 0), (0, 0), (0, LANES - 4))).reshape(DEPTH, 1, GATE_W)
    w_out_b = w_out.astype(BF16)
    rw = jnp.pad(router_w, ((0, 0), (0, 0), (0, LANES - N_EXPERTS)))
    rb = jnp.pad(router_b, ((0, 0), (0, LANES - N_EXPERTS)), constant_values=NEG).reshape(DEPTH, 1, LANES)

    cvec = jnp.concatenate([c_ctx[None, :], c, jnp.zeros((MOD_ROWS - 1 - DEC_BATCH, D_MODEL), F32)], axis=0)
    mods = _adaln(cvec, w_mod, b_mod).reshape(DEPTH * MOD_ROWS * 6, 1, D_MODEL)

    x = jnp.concatenate([x_prompt.reshape(T_CTX, D_MODEL), x_sample.reshape(T_LAT, D_MODEL)], axis=0)
    ks_l, vs_l, cs_l, ns_l, ms_l = [], [], [], [], []
    for l in range(DEPTH):
        proj, gates, k_ctx, v_ctx = _in_proj(l, x, mods, w_main, w_gate, bg)
        m_ctx, st_c, st_n, st_m = _mlstm(l, proj, gates, mlstm_norm_w, latent=False)
        (m_lat,) = _mlstm(l, proj, gates, mlstm_norm_w, latent=True,
                          state=(state_mlstm_c, state_mlstm_n, state_mlstm_m))
        a_ctx = _attention(l, proj, latent=False)
        a_lat = _attention(l, proj, latent=True, cache_k=cache_na_k, cache_v=cache_na_v, rpb=na_rpb)
        x1, h2p, top_i, top_g = _out_proj(l, m_ctx, m_lat, a_ctx, a_lat, w_out_b, x, mods, ln_w, ln_b, rw, rb)
        code, counts = _route(top_i)
        tile_start, tile_end, tables = _routing_tables(counts)
        sorted_rows = _sorted_rows(code, tile_start)
        xs = _scatter_rows(h2p, sorted_rows, tile_start, tile_end, tables[3])
        outs = _moe_experts(l, xs, tables, expert_w1, expert_b1, expert_w2, expert_b2)
        x = _combine(l, outs, sorted_rows, top_g, x1, mods, ln_w, ln_b, split=(l == DEPTH - 1))

        ks_l.append(k_ctx.reshape(BATCH, SEQ, H_NA, D_NA))
        vs_l.append(v_ctx.reshape(BATCH, SEQ, H_NA, D_NA))
        cs_l.append(st_c)
        ns_l.append(st_n.reshape(BATCH, 2, H_M, DK_M))
        ms_l.append(st_m[:, :, :, 0, 0])

    y_prompt = x[0].reshape(BATCH, SEQ, D_MODEL)
    y_sample = x[1].reshape(DEC_BATCH, DEC_SEQ, D_MODEL)
    return (y_prompt, y_sample, jnp.stack(ks_l, axis=1), jnp.stack(vs_l, axis=1),
            jnp.stack(cs_l, axis=1), jnp.stack(ns_l, axis=1), jnp.stack(ms_l, axis=1))
```

```python
import functools

import numpy as np
import jax
import jax.numpy as jnp
from jax import lax
from jax.experimental import pallas as pl
from jax.experimental.pallas import tpu as pltpu

F32 = jnp.float32
BF16 = jnp.bfloat16

D_MODEL = 2048
BATCH, SEQ = 16, 256
DEC_BATCH, DEC_SEQ = 8, 1024
DEPTH = 2
PAST_LEN = 512
GRID_W = 64
H_M, DK_M, DV_M = 4, 256, 256
H_NA, D_NA = 8, 128
MAX_KH, KW = 8, 16
N_EXPERTS, TOP_K, D_FF = 32, 4, 2048
SWIGLU_LIMIT, SWIGLU_ALPHA = 7.0, 1.702
ROPE_BASE = 10000.0
CHUNK = 128
N_GATES = 4 * H_M
W_M = H_M * DV_M
W_NA = H_NA * D_NA
ALPHA = (2 * DEPTH) ** 0.25
NEG = -1e30
LN_EPS = 1e-5
HN_EPS = 1e-6

T_CTX = BATCH * SEQ
T_LAT = DEC_BATCH * DEC_SEQ
T_ALL = T_CTX + T_LAT
N_MAIN = 4 * W_M + 3 * W_NA
N_ASSIGN = T_ALL * TOP_K

LANES = 128
VMEM_LIMIT = 56 * 1024 * 1024

TM_IN = 1024
TN_IN = 512
TM_OUT = 256
TM_ROUTE = 256
TM_MOE = 256
CH_MOE = 2048
TPC = CH_MOE // TM_MOE
TF = 256
NJ = D_FF // TF
ROWS_SORTED = N_ASSIGN + N_EXPERTS * TM_MOE
NT_SORTED = ROWS_SORTED // TM_MOE
NC_MAX = N_EXPERTS + -(-NT_SORTED // TPC)
TM_COMB = 128
D_PACK = D_MODEL // 2
RANK_BITS = 16
assert N_ASSIGN < (1 << RANK_BITS)


def _cparams(n_axes, vmem=VMEM_LIMIT):
    return pltpu.CompilerParams(dimension_semantics=("arbitrary",) * n_axes, vmem_limit_bytes=vmem)


def _dot(a, b):
    return jnp.dot(a, b, preferred_element_type=F32)


def _split2(x):
    hi = x.astype(BF16)
    lo = (x - hi.astype(F32)).astype(BF16)
    return hi, lo


def _dot_f32(a, b):
    a1, a2 = _split2(a)
    b1, b2 = _split2(b)
    return _dot(a1, b1) + (_dot(a1, b2) + _dot(a2, b1))


def _pack_bf16_pairs(x):
    n = x.shape[1] // 2
    bits = lax.bitcast_convert_type(x.astype(BF16).astype(F32), jnp.uint32)
    return (bits[:, :n] >> 16) | (bits[:, n:] & jnp.uint32(0xFFFF0000))


def _unpack_bf16_pairs(words):
    lo = lax.bitcast_convert_type(words << 16, F32)
    hi = lax.bitcast_convert_type(words & jnp.uint32(0xFFFF0000), F32)
    return lo, hi


def _layer_norm(z, w, b):
    mu = jnp.mean(z, axis=-1, keepdims=True)
    zc = z - mu
    var = jnp.mean(zc * zc, axis=-1, keepdims=True)
    return zc * lax.rsqrt(var + LN_EPS) * w + b


MOD_ROWS = 16
TN_MOD = 512


def _adaln_kernel(c_ref, w_ref, b_ref, o_ref):
    c = c_ref[...]
    s = c * jax.nn.sigmoid(c)
    o_ref[...] = _dot_f32(s, w_ref[...]) + b_ref[...]


def _adaln(cvec, w_mod, b_mod):
    n = 6 * D_MODEL
    return pl.pallas_call(
        _adaln_kernel,
        out_shape=jax.ShapeDtypeStruct((DEPTH, MOD_ROWS, n), F32),
        grid=(DEPTH, n // TN_MOD),
        in_specs=[
            pl.BlockSpec((MOD_ROWS, D_MODEL), lambda l, j: (0, 0)),
            pl.BlockSpec((None, D_MODEL, TN_MOD), lambda l, j: (l, 0, j)),
            pl.BlockSpec((None, 1, TN_MOD), lambda l, j: (l, 0, j)),
        ],
        out_specs=pl.BlockSpec((None, MOD_ROWS, TN_MOD), lambda l, j: (l, 0, j)),
        compiler_params=_cparams(2),
        name="adaln",
    )(cvec, w_mod, b_mod.reshape(DEPTH, 1, n))


def _mod_spec(layer, k, rows_per_tile):
    ctx_tiles = T_CTX // rows_per_tile
    tiles_per_lat = DEC_SEQ // rows_per_tile

    def index(i, *_):
        r = jnp.where(i < ctx_tiles, 0, 1 + (i - ctx_tiles) // tiles_per_lat)
        return ((layer * MOD_ROWS + r) * 6 + k, 0, 0)

    return pl.BlockSpec((None, 1, D_MODEL), index)


GATE_W = H_M * LANES


CTX_TILES_IN = T_CTX // TM_IN
SEQ_PER_TILE = TM_IN // SEQ
J_NK = (4 * W_M + W_NA) // TN_IN
J_NV = (4 * W_M + 2 * W_NA) // TN_IN
KV_TILES = W_NA // TN_IN


J_NA = 4 * W_M // TN_IN


def _inproj_kernel(x_ref, sh_ref, sc_ref, wa_ref, wb_ref, wg_ref, bg_ref, o_ref, g_ref, ko_ref, vo_ref, h_scr):
    i = pl.program_id(0)
    j = pl.program_id(1)

    @pl.when(j == 0)
    def _():
        h = (x_ref[...] * (1.0 + sc_ref[...]) + sh_ref[...]).astype(BF16)
        h_scr[...] = h
        g_ref[...] = _dot(h, wg_ref[...]) + bg_ref[...]

    out = _dot(h_scr[...], jnp.where(j < J_NA, wa_ref[...], wb_ref[...]))
    o_ref[...] = out
    is_ctx = i < CTX_TILES_IN

    @pl.when(is_ctx & (j >= J_NK) & (j < J_NK + KV_TILES))
    def _():
        ko_ref[...] = out.reshape(SEQ_PER_TILE, SEQ, TN_IN)

    @pl.when(is_ctx & (j >= J_NV))
    def _():
        vo_ref[...] = out.reshape(SEQ_PER_TILE, SEQ, TN_IN)


def _kv_spec(j_first):
    def index(i, j):
        jt = jnp.clip(j - j_first, 0, KV_TILES - 1)
        ctx = i < CTX_TILES_IN
        return (jnp.where(ctx, i, CTX_TILES_IN - 1), 0, jnp.where(ctx, jt, KV_TILES - 1))

    return pl.BlockSpec((SEQ_PER_TILE, SEQ, TN_IN), index)


def _in_proj(layer, x, mods, w_a, w_b, w_gate, b_gate):
    kv_shape = jax.ShapeDtypeStruct((BATCH, SEQ, W_NA), F32)
    return pl.pallas_call(
        _inproj_kernel,
        out_shape=(jax.ShapeDtypeStruct((T_ALL, N_MAIN), F32),
                   jax.ShapeDtypeStruct((T_ALL, GATE_W), F32), kv_shape, kv_shape),
        grid=(T_ALL // TM_IN, N_MAIN // TN_IN),
        in_specs=[
            pl.BlockSpec((TM_IN, D_MODEL), lambda i, j: (i, 0)),
            _mod_spec(layer, 0, TM_IN),
            _mod_spec(layer, 1, TM_IN),
            pl.BlockSpec((None, D_MODEL, TN_IN), lambda i, j: (layer, 0, jnp.minimum(j, J_NA - 1))),
            pl.BlockSpec((None, D_MODEL, TN_IN), lambda i, j: (layer, 0, jnp.maximum(j - J_NA, 0))),
            pl.BlockSpec((None, D_MODEL, GATE_W), lambda i, j: (layer, 0, 0)),
            pl.BlockSpec((None, 1, GATE_W), lambda i, j: (layer, 0, 0)),
        ],
        out_specs=(pl.BlockSpec((TM_IN, TN_IN), lambda i, j: (i, j)),
                   pl.BlockSpec((TM_IN, GATE_W), lambda i, j: (i, 0)),
                   _kv_spec(J_NK), _kv_spec(J_NV)),
        scratch_shapes=[pltpu.VMEM((TM_IN, D_MODEL), BF16)],
        compiler_params=_cparams(2),
        name="in_proj",
    )(x, mods, mods, w_a, w_b, w_gate, b_gate)


HPB = 2


def _log_sigmoid(x):
    return jnp.minimum(x, 0.0) - jnp.log(1.0 + jnp.exp(-jnp.abs(x)))


def _mlstm_kernel(*refs, seq, use_rope, has_init, emit_state, layer):
    it = iter(refs)
    q_ref, k_ref, v_ref, mo_ref, g_ref, nw_ref = (next(it) for _ in range(6))
    if use_rope:
        cos_ref, sin_ref = next(it), next(it)
    if has_init:
        c0_ref, n0_ref, m0_ref = next(it), next(it), next(it)
    out_ref = next(it)
    if emit_state:
        co_ref, no_ref, mso_ref = next(it), next(it), next(it)
    qs, ks, hs, c_scr = next(it), next(it), next(it), next(it)

    nc = seq // CHUNK
    b_idx = pl.program_id(0)
    h_idx = pl.program_id(1)

    def prep(j, carry):
        r0 = pl.multiple_of(j * CHUNK, CHUNK)
        rows = pl.ds(r0, CHUNK)
        q = q_ref[rows, :]
        k = k_ref[rows, :]
        if use_rope:
            cos = cos_ref[rows, :]
            sin = sin_ref[rows, :]

            def rope(x):
                halves = []
                for p in range(HPB * DK_M // LANES):
                    xp = x[:, p * LANES:(p + 1) * LANES]
                    t = (p % (DK_M // LANES)) * LANES
                    halves.append(xp * cos[:, t:t + LANES] + pltpu.roll(xp, LANES // 2, axis=1) * sin[:, t:t + LANES])
                return jnp.concatenate(halves, axis=1)

            q = rope(q)
            k = rope(k)
        qs[rows, :] = (q * (DK_M ** -0.5)).astype(BF16)
        ks[rows, :] = k.astype(BF16)
        return carry

    lax.fori_loop(0, nc, prep, 0)

    row_i = lax.broadcasted_iota(jnp.int32, (CHUNK, CHUNK), 0)
    col_i = lax.broadcasted_iota(jnp.int32, (CHUNK, CHUNK), 1)

    def chunk_rows(j):
        return pl.ds(pl.multiple_of(j * CHUNK, CHUNK), CHUNK)

    def head_cols(hh):
        return slice(hh * DK_M, (hh + 1) * DK_M)

    chains = [(hh, d) for hh in range(HPB) for d in range(2)]
    init = []
    for hh, d in chains:
        if has_init:
            c_scr[d, hh] = c0_ref[d, hh]
            m0 = m0_ref[((b_idx * DEPTH + layer) * 2 + d) * H_M + h_idx * HPB + hh]
            init += [n0_ref[d, hh], jnp.full((1, 1), m0, F32)]
        else:
            c_scr[d, hh] = jnp.zeros((DK_M, DV_M), F32)
            init += [jnp.zeros((1, DK_M), F32), jnp.full((1, 1), NEG, F32)]

    def chunk_step(hh, d, j, n, m):
        mask = (col_i <= row_i) if d == 0 else (col_i >= row_i)
        tri = mask.astype(BF16)
        lane_i, lane_b = 2 * d, 2 * d + 1
        end_row = CHUNK - 1 if d == 0 else 0
        rows = chunk_rows(j)
        hcols = head_cols(hh)
        gates = g_ref[rows, hh * LANES:(hh + 1) * LANES]
        lf = _log_sigmoid(gates)
        lf1, lf2 = _split2(lf)
        lf3 = (lf - lf1.astype(F32) - lf2.astype(F32)).astype(BF16)
        bc = _dot(tri, lf1) + (_dot(tri, lf2) + _dot(tri, lf3))
        gates_t = gates.T
        bc_t = bc.T
        b_col = bc[:, lane_b:lane_b + 1]
        ig_col = gates[:, lane_i:lane_i + 1]
        b_row = bc_t[lane_b:lane_b + 1, :]
        ig_row = gates_t[lane_i:lane_i + 1, :]
        b_end = bc[end_row:end_row + 1, lane_b:lane_b + 1]

        dlog = jnp.where(mask, (b_col - b_row) + ig_row, NEG)
        g = b_col + m
        mt = jnp.maximum(g, jnp.max(dlog, axis=1, keepdims=True))
        qc = qs[rows, hcols]
        kc = ks[rows, hcols]
        vc = v_ref[rows, hcols].astype(BF16)
        s = lax.dot_general(qc, kc, (((1,), (1,)), ((), ())), preferred_element_type=F32)
        s = s * jnp.exp(dlog - mt)
        sg = jnp.exp(g - mt)
        c_old = c_scr[d, hh]
        num = _dot(s.astype(BF16), vc) + sg * _dot(qc, c_old.astype(BF16))
        den = (jnp.sum(s, axis=1, keepdims=True)
               + sg * jnp.sum(qc.astype(F32) * n, axis=1, keepdims=True))
        h = num / jnp.maximum(jnp.abs(den), jnp.exp(-mt))

        wlog = (b_end - b_col) + ig_col
        m_new = jnp.maximum(b_end + m, jnp.max(wlog, axis=0, keepdims=True))
        ws = jnp.exp(wlog - m_new)
        sc = jnp.exp(b_end + m - m_new)
        kw = kc.astype(F32) * ws
        c_scr[d, hh] = sc * c_old + _dot(kw.T.astype(BF16), vc)
        n_new = sc * n + jnp.sum(kw, axis=0, keepdims=True)
        return h, n_new, m_new

    def finish(hh, j, ht):
        rows = chunk_rows(j)
        hcols = head_cols(hh)
        mu = jnp.mean(ht, axis=1, keepdims=True)
        hc = ht - mu
        var = jnp.mean(hc * hc, axis=1, keepdims=True)
        hn = hc * lax.rsqrt(var + HN_EPS) * nw_ref[:, hcols]
        out_ref[rows, hcols] = (jax.nn.sigmoid(mo_ref[rows, hcols]) * hn).astype(out_ref.dtype)

    def advance(jj, carry, complete):
        new = []
        for idx, (hh, d) in enumerate(chains):
            j = jj if d == 0 else nc - 1 - jj
            h, n, m = chunk_step(hh, d, j, carry[2 * idx], carry[2 * idx + 1])
            new += [n, m]
            if complete:
                finish(hh, j, hs[chunk_rows(j), head_cols(hh)] + h)
            else:
                hs[chunk_rows(j), head_cols(hh)] = h
        return tuple(new)

    carry = lax.fori_loop(0, nc // 2, functools.partial(advance, complete=False), tuple(init))
    carry = lax.fori_loop(nc // 2, nc, functools.partial(advance, complete=True), carry)
    if emit_state:
        for idx, (hh, d) in enumerate(chains):
            co_ref[d, hh] = c_scr[d, hh]
            no_ref[d, hh] = carry[2 * idx]
            mso_ref[d, hh] = jnp.broadcast_to(carry[2 * idx + 1], (1, LANES))


def _rope_tables(seq):
    pos = np.arange(seq)
    half = DK_M // 2
    nf = half // 2
    inv = ROPE_BASE ** (-np.arange(nf, dtype=np.float32) / nf)
    cos_parts, sin_parts = [], []
    for p in ((pos // GRID_W).astype(np.float32), (pos % GRID_W).astype(np.float32)):
        ang = (p[:, None] * inv).astype(np.float32)
        cos_parts += [np.cos(ang), np.cos(ang)]
        sin_parts += [-np.sin(ang), np.sin(ang)]
    return (jnp.asarray(np.concatenate(cos_parts, axis=1), F32),
            jnp.asarray(np.concatenate(sin_parts, axis=1), F32))


def _mlstm(layer, proj, gates, norm_w, *, latent, state=None):
    if latent:
        nb, seq, rb0 = DEC_BATCH, DEC_SEQ, T_CTX // DEC_SEQ
    else:
        nb, seq, rb0 = BATCH, SEQ, 0
    cb = W_M // (HPB * DK_M)
    wq = HPB * DK_M
    in_specs = [
        pl.BlockSpec((seq, wq), lambda b, h: (rb0 + b, h)),
        pl.BlockSpec((seq, wq), lambda b, h: (rb0 + b, cb + h)),
        pl.BlockSpec((seq, wq), lambda b, h: (rb0 + b, 2 * cb + h)),
        pl.BlockSpec((seq, wq), lambda b, h: (rb0 + b, 3 * cb + h)),
        pl.BlockSpec((seq, HPB * LANES), lambda b, h: (rb0 + b, h)),
        pl.BlockSpec((None, 1, wq), lambda b, h: (layer, 0, h)),
    ]
    args = [proj, proj, proj, proj, gates, norm_w.reshape(DEPTH, 1, W_M)]
    if latent:
        cos, sin = _rope_tables(seq)
        in_specs += [pl.BlockSpec((seq, DK_M), lambda b, h: (0, 0))] * 2
        args += [cos, sin]
        st_c, st_n, st_m = state
        in_specs += [
            pl.BlockSpec((None, None, 2, HPB, DK_M, DV_M), lambda b, h: (b, layer, 0, h, 0, 0)),
            pl.BlockSpec((None, None, 2, HPB, 1, DK_M), lambda b, h: (b, layer, 0, h, 0, 0)),
            pl.BlockSpec(memory_space=pltpu.SMEM),
        ]
        args += [st_c, st_n.reshape(DEC_BATCH, DEPTH, 2, H_M, 1, DK_M), st_m.reshape(-1)]
    out_shape = [jax.ShapeDtypeStruct((nb * seq, W_M), BF16)]
    out_specs = [pl.BlockSpec((seq, wq), lambda b, h: (b, h))]
    if not latent:
        out_shape += [
            jax.ShapeDtypeStruct((nb, 2, H_M, DK_M, DV_M), F32),
            jax.ShapeDtypeStruct((nb, 2, H_M, 1, DK_M), F32),
            jax.ShapeDtypeStruct((nb, 2, H_M, 1, LANES), F32),
        ]
        out_specs += [
            pl.BlockSpec((None, 2, HPB, DK_M, DV_M), lambda b, h: (b, 0, h, 0, 0)),
            pl.BlockSpec((None, 2, HPB, 1, DK_M), lambda b, h: (b, 0, h, 0, 0)),
            pl.BlockSpec((None, 2, HPB, 1, LANES), lambda b, h: (b, 0, h, 0, 0)),
        ]
    kern = functools.partial(_mlstm_kernel, seq=seq, use_rope=latent, has_init=latent,
                             emit_state=not latent, layer=layer)
    return pl.pallas_call(
        kern,
        out_shape=tuple(out_shape),
        grid=(nb, H_M // HPB),
        in_specs=in_specs,
        out_specs=tuple(out_specs),
        scratch_shapes=[
            pltpu.VMEM((seq, wq), BF16),
            pltpu.VMEM((seq, wq), BF16),
            pltpu.VMEM((seq, wq), F32),
            pltpu.VMEM((2, HPB, DK_M, DV_M), F32),
        ],
        compiler_params=_cparams(2),
        name="mlstm_lat" if latent else "mlstm_ctx",
    )(*args)


RPB_ROWS = 2 * MAX_KH - 1
RPB_COLS = 2 * KW - 1
Q_TILE = 256
LAT_ROWS = DEC_SEQ // GRID_W
KH = min(MAX_KH, LAT_ROWS)


def _attn_ctx_kernel(q_ref, k_ref, v_ref, o_ref):
    scale = D_NA ** -0.5
    for h in range(H_NA):
        cols = slice(h * D_NA, (h + 1) * D_NA)
        q = q_ref[:, cols].astype(BF16)
        k = k_ref[:, cols].astype(BF16)
        v = v_ref[:, cols].astype(BF16)
        s = lax.dot_general(q, k, (((1,), (1,)), ((), ())), preferred_element_type=F32) * scale
        m = jnp.max(s, axis=1, keepdims=True)
        p = jnp.exp(s - m)
        l = jnp.sum(p, axis=1, keepdims=True)
        o_ref[:, cols] = (_dot(p.astype(BF16), v) / l).astype(o_ref.dtype)


def _attn_lat_kernel(rpb_ref, q_ref, k_ref, v_ref, kc_ref, vc_ref, o_ref, bias_scr, g_scr):
    h_idx = pl.program_id(0)
    scale = D_NA ** -0.5

    @pl.when(pl.program_id(1) == 0)
    def _build_bias():
        qc = lax.broadcasted_iota(jnp.int32, (GRID_W, LANES), 0)
        lane = lax.broadcasted_iota(jnp.int32, (GRID_W, LANES), 1)
        kc = lane & (GRID_W - 1)
        upper = lane >= GRID_W
        cidx = jnp.clip(kc - qc, -(KW - 1), KW - 1) + (KW - 1)
        cs = jnp.clip(qc - KW // 2, 0, GRID_W - KW)
        col_ok = (kc >= cs) & (kc < cs + KW)
        base = h_idx * (RPB_ROWS * RPB_COLS)
        for dr0 in range(-KH, KH):
            acc = jnp.zeros((GRID_W, LANES), F32)
            for c in range(RPB_COLS):
                lo = rpb_ref[base + (dr0 + MAX_KH - 1) * RPB_COLS + c] if -MAX_KH < dr0 < MAX_KH else 0.0
                hi = rpb_ref[base + (dr0 + MAX_KH) * RPB_COLS + c] if -MAX_KH < dr0 + 1 < MAX_KH else 0.0
                acc = jnp.where(cidx == c, jnp.where(upper, hi, lo), acc)
            g_scr[dr0 + KH] = jnp.where(col_ok, acc, NEG)
        neg_tile = jnp.full((GRID_W, LANES), NEG, F32)
        for qr in range(LAT_ROWS):
            rs = min(max(qr - KH // 2, 0), LAT_ROWS - KH)
            for t in range(LAT_ROWS // 2):
                ok0 = rs <= 2 * t < rs + KH
                ok1 = rs <= 2 * t + 1 < rs + KH
                if ok0 or ok1:
                    tile = g_scr[2 * t - qr + KH]
                    if not ok1:
                        tile = jnp.where(upper, NEG, tile)
                    if not ok0:
                        tile = jnp.where(upper, tile, NEG)
                else:
                    tile = neg_tile
                bias_scr[qr * GRID_W:(qr + 1) * GRID_W, t * LANES:(t + 1) * LANES] = tile

    k = k_ref[...].astype(BF16)
    v = v_ref[...].astype(BF16)
    kc = kc_ref[...].astype(BF16)
    vc = vc_ref[...].astype(BF16)
    dn = (((1,), (1,)), ((), ()))
    for qt in range(DEC_SEQ // Q_TILE):
        rows = slice(qt * Q_TILE, (qt + 1) * Q_TILE)
        q = q_ref[rows, :].astype(BF16)
        s_loc = lax.dot_general(q, k, dn, preferred_element_type=F32) * scale + bias_scr[rows, :]
        s_ctx = lax.dot_general(q, kc, dn, preferred_element_type=F32) * scale
        m = jnp.maximum(jnp.max(s_loc, axis=1, keepdims=True), jnp.max(s_ctx, axis=1, keepdims=True))
        p_loc = jnp.exp(s_loc - m)
        p_ctx = jnp.exp(s_ctx - m)
        l = jnp.sum(p_loc, axis=1, keepdims=True) + jnp.sum(p_ctx, axis=1, keepdims=True)
        o = _dot(p_loc.astype(BF16), v) + _dot(p_ctx.astype(BF16), vc)
        o_ref[rows, :] = (o / l).astype(o_ref.dtype)


def _attention(layer, proj, *, latent, cache_k=None, cache_v=None, rpb=None):
    qb0 = 4 * W_M // D_NA
    if not latent:
        return pl.pallas_call(
            _attn_ctx_kernel,
            out_shape=jax.ShapeDtypeStruct((T_CTX, W_NA), BF16),
            grid=(BATCH,),
            in_specs=[
                pl.BlockSpec((SEQ, W_NA), lambda b: (b, qb0 // H_NA)),
                pl.BlockSpec((SEQ, W_NA), lambda b: (b, qb0 // H_NA + 1)),
                pl.BlockSpec((SEQ, W_NA), lambda b: (b, qb0 // H_NA + 2)),
            ],
            out_specs=pl.BlockSpec((SEQ, W_NA), lambda b: (b, 0)),
            compiler_params=_cparams(1),
            name="attn_ctx",
        )(proj, proj, proj)
    rb0 = T_CTX // DEC_SEQ
    ck = cache_k.reshape(DEC_BATCH, DEPTH, PAST_LEN, W_NA)
    cv = cache_v.reshape(DEC_BATCH, DEPTH, PAST_LEN, W_NA)
    grid_spec = pltpu.PrefetchScalarGridSpec(
        num_scalar_prefetch=1,
        grid=(H_NA, DEC_BATCH),
        in_specs=[
            pl.BlockSpec((DEC_SEQ, D_NA), lambda h, b, r: (rb0 + b, qb0 + h)),
            pl.BlockSpec((DEC_SEQ, D_NA), lambda h, b, r: (rb0 + b, qb0 + H_NA + h)),
            pl.BlockSpec((DEC_SEQ, D_NA), lambda h, b, r: (rb0 + b, qb0 + 2 * H_NA + h)),
            pl.BlockSpec((None, None, PAST_LEN, D_NA), lambda h, b, r: (b, layer, 0, h)),
            pl.BlockSpec((None, None, PAST_LEN, D_NA), lambda h, b, r: (b, layer, 0, h)),
        ],
        out_specs=pl.BlockSpec((DEC_SEQ, D_NA), lambda h, b, r: (b, h)),
        scratch_shapes=[
            pltpu.VMEM((DEC_SEQ, DEC_SEQ), F32),
            pltpu.VMEM((2 * KH, GRID_W, LANES), F32),
        ],
    )
    return pl.pallas_call(
        _attn_lat_kernel,
        out_shape=jax.ShapeDtypeStruct((T_LAT, W_NA), BF16),
        grid_spec=grid_spec,
        compiler_params=_cparams(2),
        name="attn_lat",
    )(rpb[layer].reshape(-1), proj, proj, proj, ck, cv)


def _outproj_kernel(mc_ref, ml_ref, ac_ref, al_ref, w_ref, x_ref, ga_ref, shf_ref, scf_ref,
                    lnw_ref, lnb_ref, rw_ref, rb_ref, x1_ref, h2_ref, ti_ref, tg_ref):
    is_ctx = pl.program_id(0) < T_CTX // TM_OUT
    m = jnp.where(is_ctx, mc_ref[...], ml_ref[...])
    a = jnp.where(is_ctx, ac_ref[...], al_ref[...])
    y = _dot(m, w_ref[0:W_M, :]) + _dot(a, w_ref[W_M:W_M + W_NA, :])
    x1 = _layer_norm(ALPHA * x_ref[...] + ga_ref[...] * y, lnw_ref[...], lnb_ref[...])
    x1_ref[...] = x1
    h2 = x1 * (1.0 + scf_ref[...]) + shf_ref[...]
    h2_ref[...] = _pack_bf16_pairs(h2)
    logits = _dot_f32(h2, rw_ref[...]) + rb_ref[...]
    lane = lax.broadcasted_iota(jnp.int32, logits.shape, 1)
    vals = logits
    top_v, top_i = [], []
    for _ in range(TOP_K):
        mx = jnp.max(vals, axis=1, keepdims=True)
        ix = jnp.min(jnp.where(vals == mx, lane, LANES), axis=1, keepdims=True)
        top_v.append(mx)
        top_i.append(ix)
        vals = jnp.where(lane == ix, -jnp.inf, vals)
    ex = [jnp.exp(v - top_v[0]) for v in top_v]
    den = ex[0] + ex[1] + ex[2] + ex[3]
    ti = jnp.zeros(logits.shape, jnp.int32)
    tg = jnp.zeros(logits.shape, F32)
    for kk in range(TOP_K):
        ti = jnp.where(lane == kk, top_i[kk], ti)
        tg = jnp.where(lane == kk, ex[kk] / den, tg)
    ti_ref[...] = ti
    tg_ref[...] = tg


def _out_proj(layer, m_ctx, m_lat, a_ctx, a_lat, w_out, x, mods, ln_w, ln_b, router_w, router_b):
    nctx = T_CTX // TM_OUT
    ctx_spec = lambda w: pl.BlockSpec((TM_OUT, w), lambda i: (jnp.minimum(i, nctx - 1), 0))
    lat_spec = lambda w: pl.BlockSpec((TM_OUT, w), lambda i: (jnp.maximum(i - nctx, 0), 0))
    row_spec = lambda w: pl.BlockSpec((TM_OUT, w), lambda i: (i, 0))
    return pl.pallas_call(
        _outproj_kernel,
        out_shape=(jax.ShapeDtypeStruct((T_ALL, D_MODEL), F32),
                   jax.ShapeDtypeStruct((T_ALL, D_PACK), jnp.uint32),
                   jax.ShapeDtypeStruct((T_ALL, LANES), jnp.int32),
                   jax.ShapeDtypeStruct((T_ALL, LANES), F32)),
        grid=(T_ALL // TM_OUT,),
        in_specs=[
            ctx_spec(W_M), lat_spec(W_M), ctx_spec(W_NA), lat_spec(W_NA),
            pl.BlockSpec((None, W_M + W_NA, D_MODEL), lambda i: (layer, 0, 0)),
            row_spec(D_MODEL),
            _mod_spec(layer, 2, TM_OUT), _mod_spec(layer, 3, TM_OUT), _mod_spec(layer, 4, TM_OUT),
            pl.BlockSpec((None, None, 1, D_MODEL), lambda i: (layer, 0, 0, 0)),
            pl.BlockSpec((None, None, 1, D_MODEL), lambda i: (layer, 0, 0, 0)),
            pl.BlockSpec((None, D_MODEL, LANES), lambda i: (layer, 0, 0)),
            pl.BlockSpec((None, 1, LANES), lambda i: (layer, 0, 0)),
        ],
        out_specs=(row_spec(D_MODEL), row_spec(D_PACK), row_spec(LANES), row_spec(LANES)),
        compiler_params=_cparams(1),
        name="out_proj",
    )(m_ctx, m_lat, a_ctx, a_lat, w_out, x, mods, mods, mods,
      ln_w.reshape(DEPTH, 2, 1, D_MODEL), ln_b.reshape(DEPTH, 2, 1, D_MODEL), router_w, router_b)


def _route_kernel(ti_ref, code_ref, cnt_ref, carry):
    @pl.when(pl.program_id(0) == 0)
    def _():
        carry[...] = jnp.zeros((1, LANES), F32)

    ti = ti_ref[...]
    lane = lax.broadcasted_iota(jnp.int32, (TM_ROUTE, LANES), 1)
    hits = [lane == ti[:, kk:kk + 1] for kk in range(TOP_K)]
    cnt = hits[0].astype(F32)
    for kk in range(1, TOP_K):
        cnt = cnt + hits[kk].astype(F32)
    row_i = lax.broadcasted_iota(jnp.int32, (TM_ROUTE, TM_ROUTE), 0)
    col_i = lax.broadcasted_iota(jnp.int32, (TM_ROUTE, TM_ROUTE), 1)
    before = _dot((col_i < row_i).astype(BF16), cnt.astype(BF16)) + carry[...]
    code = jnp.zeros((TM_ROUTE, LANES), jnp.int32)
    for kk in range(TOP_K):
        rank = jnp.sum(jnp.where(hits[kk], before, 0.0), axis=1, keepdims=True).astype(jnp.int32)
        code = jnp.where(lane == kk, (ti[:, kk:kk + 1] << RANK_BITS) | rank, code)
    code_ref[...] = code
    carry[...] = carry[...] + jnp.sum(cnt, axis=0, keepdims=True)
    cnt_ref[...] = carry[...]


def _route(top_i):
    return pl.pallas_call(
        _route_kernel,
        out_shape=(jax.ShapeDtypeStruct((T_ALL, LANES), jnp.int32),
                   jax.ShapeDtypeStruct((1, LANES), F32)),
        grid=(T_ALL // TM_ROUTE,),
        in_specs=[pl.BlockSpec((TM_ROUTE, LANES), lambda i: (i, 0))],
        out_specs=(pl.BlockSpec((TM_ROUTE, LANES), lambda i: (i, 0)),
                   pl.BlockSpec((1, LANES), lambda i: (0, 0))),
        scratch_shapes=[pltpu.VMEM((1, LANES), F32)],
        compiler_params=_cparams(1),
        name="moe_route",
    )(top_i)


def _sorted_rows(code, tile_start):
    expert = code[:, :TOP_K] >> RANK_BITS
    rank = code[:, :TOP_K] & ((1 << RANK_BITS) - 1)
    first = jnp.zeros_like(expert)
    for e in range(N_EXPERTS):
        first = jnp.where(expert == e, tile_start[e] * TM_MOE, first)
    return (first + rank).reshape(-1)


def _scatter_kernel(ts_ref, te_ref, meta_ref, row_ref, src_ref, xs_ref, zero_scr, sem):
    @pl.when(pl.program_id(0) == 0)
    def _define_padding():
        zero_scr[...] = jnp.zeros((TM_MOE, D_PACK), jnp.uint32)

        def fill(t):
            dst = xs_ref.at[pl.ds(pl.multiple_of(t * TM_MOE, TM_MOE), TM_MOE), :]
            return pltpu.make_async_copy(zero_scr, dst, sem)

        def per_expert(fn):
            def body(e, carry):
                @pl.when(te_ref[e] > ts_ref[e])
                def _():
                    fn(fill(te_ref[e] - 1))
                return carry
            lax.fori_loop(0, N_EXPERTS, body, 0)

        def tail(fn):
            def body(t, carry):
                fn(fill(t))
                return carry
            lax.fori_loop(meta_ref[1], NT_SORTED, body, 0)

        per_expert(lambda cp: cp.start())
        tail(lambda cp: cp.start())
        per_expert(lambda cp: cp.wait())
        tail(lambda cp: cp.wait())

    def start(t, carry):
        for kk in range(TOP_K):
            dst = row_ref[0, t * TOP_K + kk]
            pltpu.make_async_copy(src_ref.at[pl.ds(t, 1), :], xs_ref.at[pl.ds(dst, 1), :], sem).start(priority=kk % 2)
        return carry

    def wait(t, carry):
        for kk in range(TOP_K):
            pltpu.make_async_copy(src_ref.at[pl.ds(0, 1), :], xs_ref.at[pl.ds(0, 1), :], sem).wait()
        return carry

    lax.fori_loop(0, TM_ROUTE, start, 0, unroll=True)
    lax.fori_loop(0, TM_ROUTE, wait, 0)


def _scatter_rows(src, sorted_rows, tile_start, tile_end, meta):
    nblk = T_ALL // TM_ROUTE
    grid_spec = pltpu.PrefetchScalarGridSpec(
        num_scalar_prefetch=3,
        grid=(nblk,),
        in_specs=[
            pl.BlockSpec((None, 1, TM_ROUTE * TOP_K), lambda i, *_: (i, 0, 0), memory_space=pltpu.SMEM),
            pl.BlockSpec((TM_ROUTE, D_PACK), lambda i, *_: (i, 0)),
        ],
        out_specs=pl.BlockSpec(memory_space=pl.ANY),
        scratch_shapes=[pltpu.VMEM((TM_MOE, D_PACK), jnp.uint32), pltpu.SemaphoreType.DMA(())],
    )
    return pl.pallas_call(
        _scatter_kernel,
        out_shape=jax.ShapeDtypeStruct((ROWS_SORTED, D_PACK), jnp.uint32),
        grid_spec=grid_spec,
        compiler_params=_cparams(1),
        name="moe_scatter",
    )(tile_start, tile_end, meta, sorted_rows.reshape(nblk, 1, TM_ROUTE * TOP_K), src)


def _moe_kernel(ce_ref, cs_ref, cn_ref, nch_ref, xs_ref, w1_ref, b1_ref, w2_ref, b2_ref, sel_ref,
                out_ref, xbuf, acc, outp, stage, sem_in, sem_out):
    c = pl.program_id(0)
    j = pl.program_id(1)
    nt = cn_ref[c]
    row0 = cs_ref[c] * TM_MOE

    def tile_rows(r):
        return pl.ds(pl.multiple_of(r * TM_MOE, TM_MOE), TM_MOE)

    @pl.when(nt > 0)
    def _chunk():
        @pl.when(j == 0)
        def _load_rows():
            def fetch(r, slot):
                src = xs_ref.at[pl.ds(pl.multiple_of(row0 + r * TM_MOE, TM_MOE), TM_MOE), :]
                return pltpu.make_async_copy(src, stage.at[slot], sem_in.at[slot])

            fetch(0, 0).start()

            def load(r, carry):
                slot = r & 1
                fetch(r, slot).wait()

                @pl.when(r + 1 < nt)
                def _():
                    fetch(r + 1, 1 - slot).start()

                lo, hi = _unpack_bf16_pairs(stage[slot])
                xbuf[tile_rows(r), 0:D_PACK] = lo.astype(BF16)
                xbuf[tile_rows(r), D_PACK:D_MODEL] = hi.astype(BF16)
                acc[tile_rows(r), :] = jnp.broadcast_to(b2_ref[...], (TM_MOE, D_MODEL))
                return carry

            lax.fori_loop(0, nt, load, 0)

        def out_copy(start):
            rows = pl.ds(pl.multiple_of(start, TM_MOE), TM_MOE)
            dst = out_ref.at[pl.ds(pl.multiple_of(row0 + start, TM_MOE), TM_MOE), :]
            return pltpu.make_async_copy(outp.at[rows, :], dst, sem_out)

        def tile(start, size):
            rows = pl.ds(pl.multiple_of(start, TM_MOE), size)
            hdn = _dot(xbuf[rows, :], w1_ref[...].astype(BF16)) + b1_ref[...]
            glu = jnp.minimum(hdn, SWIGLU_LIMIT)
            lin = jnp.clip(hdn, -SWIGLU_LIMIT, SWIGLU_LIMIT) + 1.0
            z = glu * jax.nn.sigmoid(SWIGLU_ALPHA * glu) * pltpu.roll(lin, 2 * TF - 1, axis=1)
            act = _dot(z.astype(BF16), sel_ref[...]).astype(BF16)
            acc[rows, :] = acc[rows, :] + _dot(act, w2_ref[...].astype(BF16))

            @pl.when(j == NJ - 1)
            def _():
                outp[rows, :] = _pack_bf16_pairs(acc[rows, :])
                for g in range(size // TM_MOE):
                    out_copy(start + g * TM_MOE).start()

        def pair(r, carry):
            tile(r * (2 * TM_MOE), 2 * TM_MOE)
            return carry

        lax.fori_loop(0, nt >> 1, pair, 0)

        @pl.when((nt & 1) == 1)
        def _():
            tile((nt - 1) * TM_MOE, TM_MOE)

        @pl.when(j == NJ - 1)
        def _drain_row_stores():
            def wait(r, carry):
                out_copy(r * TM_MOE).wait()
                return carry

            lax.fori_loop(0, nt, wait, 0)

    @pl.when((c == NC_MAX - 1) & (j == NJ - 1))
    def _zero_unused_tiles():
        zero_rows = outp.at[pl.ds(0, TM_MOE), :]
        zero_rows[...] = jnp.zeros((TM_MOE, D_PACK), jnp.uint32)

        def copy(t):
            dst = out_ref.at[pl.ds(pl.multiple_of(t * TM_MOE, TM_MOE), TM_MOE), :]
            return pltpu.make_async_copy(zero_rows, dst, sem_out)

        def start(t, carry):
            copy(t).start()
            return carry

        def wait(t, carry):
            copy(t).wait()
            return carry

        lax.fori_loop(nch_ref[1], NT_SORTED, start, 0)
        lax.fori_loop(nch_ref[1], NT_SORTED, wait, 0)


def _moe_experts(layer, xs, tables, w1, b1, w2, b2):
    ce, cs, cn, nch = tables
    sel = np.zeros((2 * TF, TF), np.float32)
    sel[2 * np.arange(TF), np.arange(TF)] = 1.0

    def jeff(c, j, nch_ref):
        return jnp.where(c < nch_ref[0], j, NJ - 1)

    grid_spec = pltpu.PrefetchScalarGridSpec(
        num_scalar_prefetch=4,
        grid=(NC_MAX, NJ),
        in_specs=[
            pl.BlockSpec(memory_space=pl.ANY),
            pl.BlockSpec((None, None, D_MODEL, 2 * TF),
                         lambda c, j, ce, cs, cn, nch: (layer, ce[c], 0, jeff(c, j, nch))),
            pl.BlockSpec((None, None, 1, 2 * TF),
                         lambda c, j, ce, cs, cn, nch: (layer, ce[c], 0, jeff(c, j, nch))),
            pl.BlockSpec((None, None, TF, D_MODEL),
                         lambda c, j, ce, cs, cn, nch: (layer, ce[c], jeff(c, j, nch), 0)),
            pl.BlockSpec((None, None, 1, D_MODEL),
                         lambda c, j, ce, cs, cn, nch: (layer, ce[c], 0, 0)),
            pl.BlockSpec((2 * TF, TF), lambda c, j, ce, cs, cn, nch: (0, 0)),
        ],
        out_specs=pl.BlockSpec(memory_space=pl.ANY),
        scratch_shapes=[
            pltpu.VMEM((CH_MOE, D_MODEL), BF16),
            pltpu.VMEM((CH_MOE, D_MODEL), F32),
            pltpu.VMEM((CH_MOE, D_PACK), jnp.uint32),
            pltpu.VMEM((2, TM_MOE, D_PACK), jnp.uint32),
            pltpu.SemaphoreType.DMA((2,)),
            pltpu.SemaphoreType.DMA(()),
        ],
    )
    return pl.pallas_call(
        _moe_kernel,
        out_shape=jax.ShapeDtypeStruct((ROWS_SORTED, D_PACK), jnp.uint32),
        grid_spec=grid_spec,
        compiler_params=_cparams(2),
        name="moe_experts",
    )(ce, cs, cn, nch, xs, w1, b1.reshape(DEPTH, N_EXPERTS, 1, 2 * D_FF), w2,
      b2.reshape(DEPTH, N_EXPERTS, 1, D_MODEL), jnp.asarray(sel, BF16))


SUB_COMB = 16
PARTS_COMB = 4
CTX_TILES_COMB = T_CTX // TM_COMB


def _combine_kernel(row_ref, outs_ref, gw_ref, x1_ref, gf_ref, lnw_ref, lnb_ref, *rest, split):
    out_refs, (rows, sem) = rest[:-2], rest[-2:]
    part = TM_COMB // PARTS_COMB

    def start(t, carry):
        for kk in range(TOP_K):
            src = row_ref[0, t * TOP_K + kk]
            pltpu.make_async_copy(outs_ref.at[pl.ds(src, 1), :], rows.at[kk, pl.ds(t, 1), :],
                                  sem.at[lax.shift_right_logical(t, part.bit_length() - 1)]).start(priority=kk % 2)
        return carry

    lax.fori_loop(0, TM_COMB, start, 0, unroll=True)

    def emit(dst_ref):
        for p in range(PARTS_COMB):
            def wait(t, carry, p=p):
                for kk in range(TOP_K):
                    pltpu.make_async_copy(outs_ref.at[pl.ds(0, 1), :], rows.at[kk, pl.ds(0, 1), :], sem.at[p]).wait()
                return carry

            lax.fori_loop(0, part, wait, 0)

            def sub(s, carry, p=p):
                r = pl.ds(pl.multiple_of(p * part + s * SUB_COMB, SUB_COMB), SUB_COMB)
                gw = gw_ref[r, :]
                y_lo, y_hi = _unpack_bf16_pairs(rows[0, r, :])
                y_lo, y_hi = gw[:, 0:1] * y_lo, gw[:, 0:1] * y_hi
                for kk in range(1, TOP_K):
                    lo, hi = _unpack_bf16_pairs(rows[kk, r, :])
                    y_lo, y_hi = y_lo + gw[:, kk:kk + 1] * lo, y_hi + gw[:, kk:kk + 1] * hi
                y = jnp.concatenate([y_lo, y_hi], axis=1)
                dst_ref[r, :] = _layer_norm(ALPHA * x1_ref[r, :] + gf_ref[...] * y, lnw_ref[...], lnb_ref[...])
                return carry

            lax.fori_loop(0, part // SUB_COMB, sub, 0, unroll=2)

    if split:
        is_ctx = pl.program_id(0) < CTX_TILES_COMB
        pl.when(is_ctx)(lambda: emit(out_refs[0]))
        pl.when(jnp.logical_not(is_ctx))(lambda: emit(out_refs[1]))
    else:
        emit(out_refs[0])


def _combine(layer, outs, sorted_rows, gw, x1, mods, ln_w, ln_b, *, split):
    nblk = T_ALL // TM_COMB
    if split:
        out_shape = (jax.ShapeDtypeStruct((T_CTX, D_MODEL), F32), jax.ShapeDtypeStruct((T_LAT, D_MODEL), F32))
        out_specs = (
            pl.BlockSpec((TM_COMB, D_MODEL), lambda i, *_: (jnp.minimum(i, CTX_TILES_COMB - 1), 0)),
            pl.BlockSpec((TM_COMB, D_MODEL), lambda i, *_: (jnp.maximum(i - CTX_TILES_COMB, 0), 0)),
        )
    else:
        out_shape = jax.ShapeDtypeStruct((T_ALL, D_MODEL), F32)
        out_specs = pl.BlockSpec((TM_COMB, D_MODEL), lambda i, *_: (i, 0))
    grid_spec = pltpu.PrefetchScalarGridSpec(
        num_scalar_prefetch=0,
        grid=(nblk,),
        in_specs=[
            pl.BlockSpec((None, 1, TM_COMB * TOP_K), lambda i, *_: (i, 0, 0), memory_space=pltpu.SMEM),
            pl.BlockSpec(memory_space=pl.ANY),
            pl.BlockSpec((TM_COMB, LANES), lambda i, *_: (i, 0)),
            pl.BlockSpec((TM_COMB, D_MODEL), lambda i, *_: (i, 0)),
            _mod_spec(layer, 5, TM_COMB),
            pl.BlockSpec((None, None, 1, D_MODEL), lambda i, *_: (layer, 1, 0, 0)),
            pl.BlockSpec((None, None, 1, D_MODEL), lambda i, *_: (layer, 1, 0, 0)),
        ],
        out_specs=out_specs,
        scratch_shapes=[pltpu.VMEM((TOP_K, TM_COMB, D_PACK), jnp.uint32), pltpu.SemaphoreType.DMA((PARTS_COMB,))],
    )
    return pl.pallas_call(
        functools.partial(_combine_kernel, split=split),
        out_shape=out_shape,
        grid_spec=grid_spec,
        compiler_params=_cparams(1),
        name="moe_combine",
    )(sorted_rows.reshape(nblk, 1, TM_COMB * TOP_K), outs, gw, x1, mods,
      ln_w.reshape(DEPTH, 2, 1, D_MODEL), ln_b.reshape(DEPTH, 2, 1, D_MODEL))


def _routing_tables(counts_f):
    counts = counts_f[0, :N_EXPERTS].astype(jnp.int32)
    ntiles = (counts + TM_MOE - 1) // TM_MOE
    tile_end = jnp.cumsum(ntiles)
    tile_start = tile_end - ntiles
    nchunks = (ntiles + TPC - 1) // TPC
    chunk_end = jnp.cumsum(nchunks)
    total = chunk_end[-1]
    cidx = jnp.arange(NC_MAX, dtype=jnp.int32)
    last = jnp.maximum(total - 1, 0)
    ce = jnp.sum((chunk_end[None, :] <= jnp.minimum(cidx, last)[:, None]).astype(jnp.int32), axis=1)
    ce = jnp.clip(ce, 0, N_EXPERTS - 1)
    kk = cidx - (chunk_end - nchunks)[ce]
    cs = tile_start[ce] + kk * TPC
    cn = jnp.where(cidx < total, jnp.clip(ntiles[ce] - kk * TPC, 0, TPC), 0)
    cs = jnp.where(cidx < total, cs, 0)
    meta = jnp.stack([total, tile_end[-1]]).astype(jnp.int32)
    tables = (ce.astype(jnp.int32), cs.astype(jnp.int32), cn.astype(jnp.int32), meta)
    return tile_start.astype(jnp.int32), tile_end.astype(jnp.int32), tables


def kernel(x_prompt, x_sample, c, cache_na_k, cache_na_v, state_mlstm_c, state_mlstm_n, state_mlstm_m,
           c_ctx, w_mod, b_mod, w_in, b_gate, mlstm_norm_w, na_rpb, w_out, ln_w, ln_b,
           router_w, router_b, expert_w1, expert_b1, expert_w2, expert_b2):
    g0 = 4 * W_M
    w_a = w_in[:, :, :g0].astype(BF16)
    w_b = w_in[:, :, g0 + N_GATES:].astype(BF16)
    wg = w_in[:, :, g0:g0 + N_GATES].reshape(DEPTH, D_MODEL, 4, H_M).transpose(0, 1, 3, 2)
    w_gate = jnp.pad(wg, ((0, 0), (0, 0), (0, 0), (0, LANES - 4))).reshape(DEPTH, D_MODEL, GATE_W).astype(BF16)
    bg = b_gate.reshape(DEPTH, 4, H_M).transpose(0, 2, 1)
    bg = jnp.pad(bg, ((0, 0), (0, 0), (0, LANES - 4))).reshape(DEPTH, 1, GATE_W)
    w_out_b = w_out.astype(BF16)
    rw = jnp.pad(router_w, ((0, 0), (0, 0), (0, LANES - N_EXPERTS)))
    rb = jnp.pad(router_b, ((0, 0), (0, LANES - N_EXPERTS)), constant_values=NEG).reshape(DEPTH, 1, LANES)

    cvec = jnp.concatenate([c_ctx[None, :], c, jnp.zeros((MOD_ROWS - 1 - DEC_BATCH, D_MODEL), F32)], axis=0)
    mods = _adaln(cvec, w_mod, b_mod).reshape(DEPTH * MOD_ROWS * 6, 1, D_MODEL)

    x = jnp.concatenate([x_prompt.reshape(T_CTX, D_MODEL), x_sample.reshape(T_LAT, D_MODEL)], axis=0)
    ks_l, vs_l, cs_l, ns_l, ms_l = [], [], [], [], []
    for l in range(DEPTH):
        proj, gates, k_ctx, v_ctx = _in_proj(l, x, mods, w_a, w_b, w_gate, bg)
        m_ctx, st_c, st_n, st_m = _mlstm(l, proj, gates, mlstm_norm_w, latent=False)
        (m_lat,) = _mlstm(l, proj, gates, mlstm_norm_w, latent=True,
                          state=(state_mlstm_c, state_mlstm_n, state_mlstm_m))
        a_ctx = _attention(l, proj, latent=False)
        a_lat = _attention(l, proj, latent=True, cache_k=cache_na_k, cache_v=cache_na_v, rpb=na_rpb)
        x1, h2p, top_i, top_g = _out_proj(l, m_ctx, m_lat, a_ctx, a_lat, w_out_b, x, mods, ln_w, ln_b, rw, rb)
        code, counts = _route(top_i)
        tile_start, tile_end, tables = _routing_tables(counts)
        sorted_rows = _sorted_rows(code, tile_start)
        xs = _scatter_rows(h2p, sorted_rows, tile_start, tile_end, tables[3])
        outs = _moe_experts(l, xs, tables, expert_w1, expert_b1, expert_w2, expert_b2)
        x = _combine(l, outs, sorted_rows, top_g, x1, mods, ln_w, ln_b, split=(l == DEPTH - 1))

        ks_l.append(k_ctx.reshape(BATCH, SEQ, H_NA, D_NA))
        vs_l.append(v_ctx.reshape(BATCH, SEQ, H_NA, D_NA))
        cs_l.append(st_c)
        ns_l.append(st_n.reshape(BATCH, 2, H_M, DK_M))
        ms_l.append(st_m[:, :, :, 0, 0])

    y_prompt = x[0].reshape(BATCH, SEQ, D_MODEL)
    y_sample = x[1].reshape(DEC_BATCH, DEC_SEQ, D_MODEL)
    return (y_prompt, y_sample, jnp.stack(ks_l, axis=1), jnp.stack(vs_l, axis=1),
            jnp.stack(cs_l, axis=1), jnp.stack(ns_l, axis=1), jnp.stack(ms_l, axis=1))
```
